```python
import math
import jax, jax.numpy as jnp
from jax import lax
import numpy as np


D_MODEL = 1024
BATCH = 2
SEQ = 16384
DEPTH = 2

HEAD_DIM = 64
A_HEADS = 8
A_KV = 2
B_HEADS = 8
B_KV = 2
C_HEADS = 4
C_KV = 2
C_V_DIM = 2 * HEAD_DIM
WINDOW = 128
BLOCK = 128
GRID_W = 64
ROPE_THETA = 10000.0
NUM_BUCKETS = 32
MAX_DISTANCE = 128
N_BIAS_HEADS = B_HEADS + C_HEADS
D_FF = 2816
N_BRANCH = 3
BRANCH_WIDTH = 512
EPS = 1e-6

A_Q_W = A_HEADS * HEAD_DIM
A_KV_W = A_KV * HEAD_DIM
B_Q_W = B_HEADS * HEAD_DIM
B_KV_W = B_KV * HEAD_DIM
C_Q_W = C_HEADS * 2 * HEAD_DIM
C_K_W = C_KV * 2 * HEAD_DIM
C_V_W = C_KV * C_V_DIM
IN_SIZES = (A_Q_W, A_KV_W, A_KV_W, B_Q_W, B_KV_W, B_KV_W, C_Q_W, C_K_W, C_V_W, N_BRANCH * D_MODEL)
IN_COLS = sum(IN_SIZES)

kernel_name = "hybrid_gated_axial_window_diff_attn_encoder"


def rmsnorm(x, g):
    xf = x.astype(jnp.float32)
    r = lax.rsqrt(jnp.mean(xf * xf, axis=-1, keepdims=True) + EPS)
    return (xf * r * g.astype(jnp.float32)).astype(x.dtype)


def swiglu(h, w_in, w_out):
    g, u = jnp.split(h @ w_in, 2, axis=-1)
    return (jax.nn.silu(g) * u) @ w_out


def t5_bucket(rel):
    nb = NUM_BUCKETS // 2
    max_exact = nb // 2
    side = jnp.where(rel > 0, nb, 0)
    n = jnp.abs(rel)
    nf = jnp.maximum(n, 1).astype(jnp.float32)
    large = max_exact + (jnp.log(nf / max_exact) / math.log(MAX_DISTANCE / max_exact) * (nb - max_exact)).astype(jnp.int32)
    large = jnp.minimum(large, nb - 1)
    return side + jnp.where(n < max_exact, n, large)


def axial_rope_tables(seq):
    rows = seq // GRID_W
    row_ids = jnp.repeat(jnp.arange(rows), GRID_W).astype(jnp.float32)
    col_ids = jnp.tile(jnp.arange(GRID_W), rows).astype(jnp.float32)
    half = HEAD_DIM // 2
    freqs = ROPE_THETA ** (-jnp.arange(0, half, 2, dtype=jnp.float32) / half)
    ang_r = row_ids[:, None] * freqs
    ang_c = col_ids[:, None] * freqs
    return (jnp.cos(ang_r), jnp.sin(ang_r), jnp.cos(ang_c), jnp.sin(ang_c))


def rope_1d(x, cos, sin):
    x1, x2 = jnp.split(x, 2, axis=-1)
    c = cos[None, :, None, :]
    s = sin[None, :, None, :]
    return jnp.concatenate([x1 * c - x2 * s, x1 * s + x2 * c], axis=-1)


def axial_rope(x, tables):
    cos_r, sin_r, cos_c, sin_c = tables
    xf = x.astype(jnp.float32)
    half = HEAD_DIM // 2
    out = jnp.concatenate([rope_1d(xf[..., :half], cos_r, sin_r), rope_1d(xf[..., half:], cos_c, sin_c)], axis=-1)
    return out.astype(x.dtype)


def to_blocks(t):
    b, s = t.shape[:2]
    return jnp.moveaxis(t.reshape((b, s // BLOCK, BLOCK) + t.shape[2:]), 1, 0)


def mixer_a(q, k, v, qg, kg, rope):
    b, s = q.shape[:2]
    scale = HEAD_DIM ** -0.5
    q = axial_rope(rmsnorm(q, qg), rope)
    k = axial_rope(rmsnorm(k, kg), rope)
    qb = to_blocks(q.reshape(b, s, A_KV, A_HEADS // A_KV, HEAD_DIM) * scale)

    def one_block(qblk):
        sc = jnp.einsum('bqkgd,bskd->bkgqs', qblk, k).astype(jnp.float32)
        p = jax.nn.softmax(sc, axis=-1).astype(v.dtype)
        return jnp.einsum('bkgqs,bskd->bqkgd', p, v)

    o = lax.map(one_block, qb)
    return jnp.moveaxis(o, 0, 1).reshape(b, s, A_HEADS * HEAD_DIM)


def window_bias_and_mask(rel_bias, seq):
    nb = seq // BLOCK
    i = jnp.arange(BLOCK)[:, None]
    j = jnp.arange(3 * BLOCK)[None, :]
    rel = j - BLOCK - i
    in_window = jnp.abs(rel) <= WINDOW
    key_pos = jnp.arange(nb)[:, None] * BLOCK - BLOCK + jnp.arange(3 * BLOCK)[None, :]
    in_range = (key_pos >= 0) & (key_pos < seq)
    valid = in_window[None] & in_range[:, None, :]
    bias = rel_bias[:, :B_HEADS][t5_bucket(rel)].astype(jnp.float32)
    bias = bias.reshape(BLOCK, 3 * BLOCK, B_KV, B_HEADS // B_KV).transpose(2, 3, 0, 1)
    return bias, valid


def mixer_b(q, k, v, sink, bias, valid):
    b, s = q.shape[:2]
    nb = s // BLOCK
    scale = HEAD_DIM ** -0.5
    pad = ((0, 0), (BLOCK, BLOCK), (0, 0), (0, 0))
    kp = jnp.pad(k, pad).reshape(b, nb + 2, BLOCK, B_KV, HEAD_DIM)
    vp = jnp.pad(v, pad).reshape(b, nb + 2, BLOCK, B_KV, HEAD_DIM)
    kband = jnp.concatenate([kp[:, :-2], kp[:, 1:-1], kp[:, 2:]], axis=2)
    vband = jnp.concatenate([vp[:, :-2], vp[:, 1:-1], vp[:, 2:]], axis=2)
    qb = q.reshape(b, nb, BLOCK, B_KV, B_HEADS // B_KV, HEAD_DIM) * scale
    sc = jnp.einsum('bnqkgd,bnskd->bnkgqs', qb, kband).astype(jnp.float32) + bias[None, None]
    sc = jnp.where(valid[None, :, None, None], sc, -1e30)
    sk = sink.astype(jnp.float32).reshape(B_KV, B_HEADS // B_KV)[None, None, :, :, None, None]
    m = jnp.maximum(jnp.max(sc, axis=-1, keepdims=True), sk)
    e = jnp.exp(sc - m)
    p = e / (jnp.sum(e, axis=-1, keepdims=True) + jnp.exp(sk - m))
    o = jnp.einsum('bnkgqs,bnskd->bnqkgd', p.astype(v.dtype), vband)
    return o.reshape(b, s, B_HEADS * HEAD_DIM)


def mixer_c(q, k, v, lq1, lk1, lq2, lk2, subln_g, lambda_init, rel_bias):
    b, s = q.shape[:2]
    nb = s // BLOCK
    g = C_HEADS // C_KV
    scale = HEAD_DIM ** -0.5
    f32 = jnp.float32
    lam = (jnp.exp(jnp.sum(lq1.astype(f32) * lk1.astype(f32))) - jnp.exp(jnp.sum(lq2.astype(f32) * lk2.astype(f32))) + lambda_init)
    table = rel_bias[:, B_HEADS:]
    qb = to_blocks(q.reshape(b, s, C_KV, g, 2, HEAD_DIM) * scale)
    starts = jnp.arange(nb) * BLOCK
    key_pos = jnp.arange(s)

    def one_block(args):
        qblk, q0 = args
        sc = jnp.einsum('bqkgmd,bskmd->bkgmqs', qblk, k).astype(f32)
        rel = key_pos[None, :] - (q0 + jnp.arange(BLOCK))[:, None]
        bias = table[t5_bucket(rel)].astype(f32).reshape(BLOCK, s, C_KV, g).transpose(2, 3, 0, 1)
        p = jax.nn.softmax(sc + bias[None, :, :, None], axis=-1)
        attn = p[:, :, :, 0] - lam * p[:, :, :, 1]
        return jnp.einsum('bkgqs,bskd->bqkgd', attn.astype(v.dtype), v)

    o = lax.map(one_block, (qb, starts))
    o = jnp.moveaxis(o, 0, 1).reshape(b, s, C_HEADS, C_V_DIM)
    o = rmsnorm(o, subln_g) * (1.0 - lambda_init)
    return o.reshape(b, s, C_HEADS * C_V_DIM)


def setup_inputs(seed: int = 0) -> dict:
    key = jax.random.key(seed)
    ks = jax.random.split(key, 24)
    f32 = jnp.float32
    nrm = lambda k, shape, sc: jax.random.normal(k, shape, f32) * sc
    gain = lambda k, shape: 1.0 + 0.02 * jax.random.normal(k, shape, f32)
    return {
        "x": jax.random.normal(ks[0], (BATCH, SEQ, D_MODEL), f32),
        "rel_bias": nrm(ks[1], (NUM_BUCKETS, N_BIAS_HEADS), 0.5),
        "norm_ffn1": gain(ks[2], (DEPTH, D_MODEL)),
        "w_ffn1_in": nrm(ks[3], (DEPTH, D_MODEL, 2 * D_FF), D_MODEL ** -0.5),
        "w_ffn1_out": nrm(ks[4], (DEPTH, D_FF, D_MODEL), D_FF ** -0.5),
        "norm_mix": gain(ks[5], (DEPTH, D_MODEL)),
        "w_in": nrm(ks[6], (DEPTH, D_MODEL, IN_COLS), D_MODEL ** -0.5),
        "qnorm_a": gain(ks[7], (DEPTH, HEAD_DIM)),
        "knorm_a": gain(ks[8], (DEPTH, HEAD_DIM)),
        "sink_b": nrm(ks[9], (DEPTH, B_HEADS), 0.5),
        "lam_q1": nrm(ks[10], (DEPTH, HEAD_DIM), 0.1),
        "lam_k1": nrm(ks[11], (DEPTH, HEAD_DIM), 0.1),
        "lam_q2": nrm(ks[12], (DEPTH, HEAD_DIM), 0.1),
        "lam_k2": nrm(ks[13], (DEPTH, HEAD_DIM), 0.1),
        "subln_c": gain(ks[14], (DEPTH, C_V_DIM)),
        "w_branch": nrm(ks[15], (DEPTH, N_BRANCH, BRANCH_WIDTH, D_MODEL), BRANCH_WIDTH ** -0.5),
        "w_out": nrm(ks[16], (DEPTH, D_MODEL, D_MODEL), D_MODEL ** -0.5),
        "norm_ffn2": gain(ks[17], (DEPTH, D_MODEL)),
        "w_ffn2_in": nrm(ks[18], (DEPTH, D_MODEL, 2 * D_FF), D_MODEL ** -0.5),
        "w_ffn2_out": nrm(ks[19], (DEPTH, D_FF, D_MODEL), D_FF ** -0.5),
        "norm_final": gain(ks[20], (D_MODEL,)),
    }


def reference(x, rel_bias, norm_ffn1, w_ffn1_in, w_ffn1_out, norm_mix, w_in, qnorm_a, knorm_a, sink_b, lam_q1, lam_k1, lam_q2, lam_k2, subln_c, w_branch, w_out, norm_ffn2, w_ffn2_in, w_ffn2_out, norm_final):
    b, s, _ = x.shape
    rope = axial_rope_tables(s)
    win_bias, win_valid = window_bias_and_mask(rel_bias, s)
    split_points = np.cumsum(IN_SIZES)[:-1].tolist()
    for l in range(DEPTH):
        lambda_init = 0.8 - 0.6 * math.exp(-0.3 * l)
        x = x + 0.5 * swiglu(rmsnorm(x, norm_ffn1[l]), w_ffn1_in[l], w_ffn1_out[l])
        h = rmsnorm(x, norm_mix[l])
        qa, ka, va, qb_, kb, vb, qc, kc, vc, gate_logits = jnp.split(h @ w_in[l], split_points, axis=-1)
        y_a = mixer_a(qa.reshape(b, s, A_HEADS, HEAD_DIM), ka.reshape(b, s, A_KV, HEAD_DIM), va.reshape(b, s, A_KV, HEAD_DIM), qnorm_a[l], knorm_a[l], rope)
        y_b = mixer_b(qb_.reshape(b, s, B_HEADS, HEAD_DIM), kb.reshape(b, s, B_KV, HEAD_DIM), vb.reshape(b, s, B_KV, HEAD_DIM), sink_b[l], win_bias, win_valid)
        y_c = mixer_c(qc.reshape(b, s, C_HEADS, 2, HEAD_DIM), kc.reshape(b, s, C_KV, 2, HEAD_DIM), vc.reshape(b, s, C_KV, C_V_DIM), lam_q1[l], lam_k1[l], lam_q2[l], lam_k2[l], subln_c[l], lambda_init, rel_bias)
        ys = jnp.stack([y_a, y_b, y_c], axis=2)
        branches = jnp.einsum('bsnw,nwd->bsnd', ys, w_branch[l])
        gates = jax.nn.sigmoid(gate_logits.reshape(b, s, N_BRANCH, D_MODEL))
        merged = jnp.sum(gates * branches, axis=2)
        x = x + merged @ w_out[l]
        x = x + 0.5 * swiglu(rmsnorm(x, norm_ffn2[l]), w_ffn2_in[l], w_ffn2_out[l])
    return rmsnorm(x, norm_final)
```

```python
import functools
import math

import jax
import jax.numpy as jnp
import numpy as np
from jax import lax
from jax.experimental import pallas as pl
from jax.experimental.pallas import tpu as pltpu

F32 = jnp.float32
BF16 = jnp.bfloat16

HEAD_DIM = 64
A_HEADS, A_KV = 8, 2
B_HEADS, B_KV = 8, 2
C_HEADS, C_KV = 4, 2
C_V_DIM = 2 * HEAD_DIM
WINDOW = 128
BLOCK = 128
GRID_W = 64
ROPE_THETA = 10000.0
NUM_BUCKETS = 32
MAX_DISTANCE = 128
N_BRANCH = 3
BRANCH_WIDTH = 512
EPS = 1e-6
NEG_BIG = -1e30
LOG2E = math.log2(math.e)
SCALE = HEAD_DIM ** -0.5

QA0, KA0, VA0 = 0, 512, 640
QB0, KB0, VB0 = 768, 1280, 1408
QC0, KC0, VC0 = 1536, 2048, 2304
QKV_ROWS = 2560

ONES_ROWS = 16
VMEM_LIMIT_BYTES = 56 * 1024 * 1024


def _params(semantics):
    return pltpu.CompilerParams(dimension_semantics=semantics, vmem_limit_bytes=VMEM_LIMIT_BYTES)


def _const_spec(shape):
    return pl.BlockSpec(shape, lambda *_: (0,) * len(shape), pipeline_mode=pl.Buffered(1))


def _rms_rows(x, gain):
    r = lax.rsqrt(jnp.mean(x * x, axis=-1, keepdims=True) + EPS)
    return x * r * gain


def _ffn_body(x_ref, g_ref, wg_ref, wu_ref, wo_ref, gf_ref, o_ref, *, chunks, final_norm):
    x = x_ref[0]
    h = _rms_rows(x, g_ref[...]).astype(BF16)
    acc = None
    for c0, cw in chunks:
        gg = jnp.dot(h, wg_ref[:, c0:c0 + cw], preferred_element_type=F32)
        uu = jnp.dot(h, wu_ref[:, c0:c0 + cw], preferred_element_type=F32)
        act = (gg * (1.0 / (1.0 + jnp.exp(-gg))) * uu).astype(BF16)
        part = jnp.dot(act, wo_ref[c0:c0 + cw, :], preferred_element_type=F32)
        acc = part if acc is None else acc + part
    y = x + 0.5 * acc
    if final_norm:
        y = _rms_rows(y, gf_ref[...])
    o_ref[0] = y


def _ffn(x, gain, w_in, w_out, final_gain=None, *, tm):
    b, s, d = x.shape
    f = w_out.shape[0]
    wg = w_in[:, :f].astype(BF16)
    wu = w_in[:, f:].astype(BF16)
    wo = w_out.astype(BF16)
    half = f // 2
    chunks = ((0, half), (half, f - half)) if half % 128 == 0 else ((0, f),)
    final_norm = final_gain is not None
    gf = (final_gain if final_norm else gain).reshape(1, d)
    const = _const_spec
    return pl.pallas_call(
        functools.partial(_ffn_body, chunks=chunks, final_norm=final_norm),
        grid=(b, s // tm),
        in_specs=[
            pl.BlockSpec((1, tm, d), lambda bi, i: (bi, i, 0)),
            const((1, d)), const((d, f)), const((d, f)), const((f, d)), const((1, d)),
        ],
        out_specs=pl.BlockSpec((1, tm, d), lambda bi, i: (bi, i, 0)),
        out_shape=jax.ShapeDtypeStruct((b, s, d), F32),
        compiler_params=_params(("parallel", "parallel")),
        name="ffn",
    )(x, gain.reshape(1, d), wg, wu, wo, gf)


def _rope_rows(x, tab):
    cr, sr, cc, sc = tab[0:16], tab[16:32], tab[32:48], tab[48:64]
    x1r, x2r, x1c, x2c = x[0:16], x[16:32], x[32:48], x[48:64]
    return jnp.concatenate(
        [x1r * cr - x2r * sr, x1r * sr + x2r * cr, x1c * cc - x2c * sc, x1c * sc + x2c * cc], axis=0)


def _head_norm_rope(x, gain, tab):
    r = lax.rsqrt(jnp.mean(x * x, axis=0, keepdims=True) + EPS)
    return _rope_rows(x * r * gain, tab)


def _inproj_body(x_ref, g_ref, wt_ref, qg_ref, kg_ref, tab_ref, o_ref):
    h = _rms_rows(x_ref[0], g_ref[...]).astype(BF16)
    yt = lax.dot_general(wt_ref[...], h, (((1,), (1,)), ((), ())),
                         preferred_element_type=F32)
    tab = tab_ref[...]
    qscale = SCALE * LOG2E
    for hd in range(A_HEADS):
        r0 = QA0 + hd * HEAD_DIM
        q = _head_norm_rope(yt[r0:r0 + HEAD_DIM], qg_ref[...], tab)
        o_ref[0, r0:r0 + HEAD_DIM, :] = (q * qscale).astype(BF16)
    for hd in range(A_KV):
        r0 = KA0 + hd * HEAD_DIM
        o_ref[0, r0:r0 + HEAD_DIM, :] = _head_norm_rope(yt[r0:r0 + HEAD_DIM], kg_ref[...], tab).astype(BF16)
    o_ref[0, VA0:QB0, :] = yt[VA0:QB0].astype(BF16)
    o_ref[0, QB0:KB0, :] = (yt[QB0:KB0] * qscale).astype(BF16)
    o_ref[0, KB0:QC0, :] = yt[KB0:QC0].astype(BF16)
    o_ref[0, QC0:KC0, :] = (yt[QC0:KC0] * qscale).astype(BF16)
    o_ref[0, KC0:QKV_ROWS, :] = yt[KC0:QKV_ROWS].astype(BF16)


def _inproj(x, gain, w_qkv_t, qgain, kgain, rope_tab, *, tm):
    b, s, d = x.shape
    const = _const_spec
    return pl.pallas_call(
        _inproj_body,
        grid=(b, s // tm),
        in_specs=[
            pl.BlockSpec((1, tm, d), lambda bi, i: (bi, i, 0)),
            const((1, d)), const((QKV_ROWS, d)), const((HEAD_DIM, tm)), const((HEAD_DIM, tm)),
            pl.BlockSpec((HEAD_DIM, tm), lambda bi, i: (0, i)),
        ],
        out_specs=pl.BlockSpec((1, QKV_ROWS, tm), lambda bi, i: (bi, 0, i)),
        out_shape=jax.ShapeDtypeStruct((b, QKV_ROWS, s), BF16),
        compiler_params=_params(("parallel", "parallel")),
        name="inproj",
    )(x, gain.reshape(1, d), w_qkv_t,
      jnp.broadcast_to(qgain.reshape(HEAD_DIM, 1), (HEAD_DIM, tm)),
      jnp.broadcast_to(kgain.reshape(HEAD_DIM, 1), (HEAD_DIM, tm)),
      rope_tab)


def _dense_attn_body(*refs, dv, tq, tk, n_k, diff, near_lo, near_hi, near_step, lambda_init):
    if diff:
        (q_ref, k_ref, v_ref, bias_ref, cfar_ref, lam_ref, sg_ref, o_ref, acc_ref, m_ref) = refs
    else:
        (q_ref, k_ref, v_ref, o_ref, acc_ref, m_ref) = refs
    n_maps = 4
    qi = pl.program_id(2)
    kvh = pl.program_id(1)

    acc_ref[...] = jnp.zeros(acc_ref.shape, F32)
    m_ref[...] = jnp.full(m_ref.shape, NEG_BIG, F32)
    ones = jnp.ones((ONES_ROWS, tk), BF16)

    def chunk(j, side):
        ks = pl.multiple_of(j * tk, tk)
        v_aug = jnp.concatenate([v_ref[0, :, pl.ds(ks, tk)], ones], axis=0)
        for mp in range(n_maps):
            if diff:
                kmat = k_ref[0, 0, mp % 2, pl.ds(ks, tk), :]
            else:
                kmat = k_ref[0, 0, pl.ds(ks, tk), :]
            s = jnp.dot(kmat, q_ref[0, mp * HEAD_DIM:(mp + 1) * HEAD_DIM, :],
                        preferred_element_type=F32)
            if side == "near":
                di = lax.div(j * tk - qi * tq - near_lo, near_step)
                s = s + bias_ref[mp // 2, di]
            mx = jnp.max(s, axis=0, keepdims=True)
            if side in (0, 1):
                c = cfar_ref[kvh * 2 + mp // 2, side]
                mx = mx + c
            m_old = m_ref[mp]
            m_new = jnp.maximum(m_old, mx)
            alpha = jnp.exp2(m_old - m_new)
            off = m_new - c if side in (0, 1) else m_new
            p = jnp.exp2(s - off).astype(BF16)
            pv = jnp.dot(v_aug, p, preferred_element_type=F32)
            acc_ref[mp] = alpha * acc_ref[mp] + pv
            m_ref[mp] = m_new

    def run(lo, hi, side):
        def body(j, carry):
            chunk(j, side)
            return carry
        lax.fori_loop(lo, hi, body, 0)

    if diff:
        q0 = qi * tq
        pad = n_k + 2
        j0 = jnp.clip(lax.div(q0 + near_lo + pad * tk + tk - 1, tk) - pad, 0, n_k)
        j1 = jnp.clip(lax.div(q0 + near_hi + pad * tk, tk) - pad + 1, 0, n_k)
        run(0, j0, 0)
        run(j0, j1, "near")
        run(j1, n_k, 1)
    else:
        run(0, n_k, None)

    if diff:
        lam = (jnp.exp(jnp.sum(lam_ref[0:1] * lam_ref[1:2], axis=-1, keepdims=True))
               - jnp.exp(jnp.sum(lam_ref[2:3] * lam_ref[3:4], axis=-1, keepdims=True)) + lambda_init)
        outs = []
        for hh in range(2):
            a0 = acc_ref[2 * hh]
            a1 = acc_ref[2 * hh + 1]
            o = a0[0:dv] / a0[dv:dv + 1] - lam * (a1[0:dv] / a1[dv:dv + 1])
            r = lax.rsqrt(jnp.mean(o * o, axis=0, keepdims=True) + EPS)
            outs.append(o * r * sg_ref[...] * (1.0 - lambda_init))
        ot = jnp.concatenate(outs, axis=0)
    else:
        outs = []
        for mp in range(n_maps):
            a = acc_ref[mp]
            outs.append(a[0:dv] / a[dv:dv + 1])
        ot = jnp.concatenate(outs, axis=0)
    o_ref[0] = ot.T.astype(BF16)


def _near_offsets(tq, tk):
    g = math.gcd(tq, tk)
    ds = [d for d in range(-(tk // g + 1) * g, (tq // g + 2) * g, g)
          if d - (tq - 1) <= MAX_DISTANCE - 1 and d + (tk - 1) >= -(MAX_DISTANCE - 1)]
    return ds[0], ds[-1], g


def _dense_attn(qkv, k_nat, *, q_row0, v_row0, dv, tq, tk, diff_args=None):
    b, _, s = qkv.shape
    n_k = s // tk
    diff = diff_args is not None
    qblk = q_row0 // 256
    vblk = v_row0 // dv
    in_specs = [
        pl.BlockSpec((1, 256, tq), lambda bi, g, i: (bi, qblk + g, i)),
        (pl.BlockSpec((1, 1, 2, s, HEAD_DIM), lambda bi, g, i: (bi, g, 0, 0, 0)) if diff
         else pl.BlockSpec((1, 1, s, HEAD_DIM), lambda bi, g, i: (bi, g, 0, 0))),
        pl.BlockSpec((1, dv, s), lambda bi, g, i: (bi, vblk + g, 0)),
    ]
    args = [qkv, k_nat, qkv]
    kw = dict(near_lo=0, near_hi=0, near_step=1, lambda_init=0.0)
    if diff:
        bias_tiles, cfar, lam_vecs, subln, lambda_init = diff_args
        lo, hi, g_ = _near_offsets(tq, tk)
        n_d = (hi - lo) // g_ + 1
        kw = dict(near_lo=lo, near_hi=hi, near_step=g_, lambda_init=lambda_init)
        in_specs += [
            pl.BlockSpec((2, n_d, tk, tq), lambda bi, g, i: (g, 0, 0, 0)),
            pl.BlockSpec(memory_space=pltpu.SMEM),
            pl.BlockSpec((4, HEAD_DIM), lambda bi, g, i: (0, 0)),
            pl.BlockSpec((dv, tq), lambda bi, g, i: (0, 0)),
        ]
        args += [bias_tiles, cfar, lam_vecs, jnp.broadcast_to(subln.reshape(dv, 1), (dv, tq))]
    return pl.pallas_call(
        functools.partial(_dense_attn_body, dv=dv, tq=tq, tk=tk, n_k=n_k, diff=diff, **kw),
        grid=(b, 2, s // tq),
        in_specs=in_specs,
        out_specs=pl.BlockSpec((1, tq, 256), lambda bi, g, i: (bi, i, g)),
        out_shape=jax.ShapeDtypeStruct((b, s, BRANCH_WIDTH), BF16),
        scratch_shapes=[pltpu.VMEM((4, dv + ONES_ROWS, tq), F32), pltpu.VMEM((4, 1, tq), F32)],
        compiler_params=_params(("parallel", "parallel", "parallel")),
        name="attn_diff" if diff else "attn_axial",
    )(*args)


def _window_body(q_ref, kp_ref, kc_ref, kn_ref, vp_ref, vc_ref, vn_ref, bias_ref, sink_ref, o_ref, *, n_blocks):
    n = pl.program_id(2)
    qt = q_ref[0]
    q_all = jnp.concatenate([qt[h * HEAD_DIM:(h + 1) * HEAD_DIM] for h in range(4)], axis=1)
    kband = jnp.concatenate([kp_ref[0, 0], kc_ref[0, 0], kn_ref[0, 0]], axis=0)
    vband = jnp.concatenate([vp_ref[0], vc_ref[0], vn_ref[0]], axis=1)
    s = jnp.dot(kband, q_all, preferred_element_type=F32) + bias_ref[0]
    row = lax.broadcasted_iota(jnp.int32, s.shape, 0)
    valid = jnp.logical_and(jnp.logical_or(n > 0, row >= BLOCK),
                            jnp.logical_or(n < n_blocks - 1, row < 2 * BLOCK))
    s = jnp.where(valid, s, NEG_BIG)
    sk = sink_ref[0]
    m = jnp.maximum(jnp.max(s, axis=0, keepdims=True), sk)
    e = jnp.exp2(s - m)
    denom = jnp.sum(e, axis=0, keepdims=True) + jnp.exp2(sk - m)
    p = (e / denom).astype(BF16)
    outs = [jnp.dot(vband, p[:, h * BLOCK:(h + 1) * BLOCK], preferred_element_type=F32) for h in range(4)]
    o_ref[0] = jnp.concatenate(outs, axis=0).T.astype(BF16)


def _window_attn(qkv, k_nat, bias, sink):
    b, _, s = qkv.shape
    nb = s // BLOCK
    qblk = QB0 // 256
    vblk = VB0 // HEAD_DIM
    prev = lambda n: jnp.maximum(n - 1, 0)
    nxt = lambda n: jnp.minimum(n + 1, nb - 1)
    kspec = lambda f: pl.BlockSpec((1, 1, BLOCK, HEAD_DIM), lambda bi, g, n: (bi, g, f(n), 0))
    vspec = lambda f: pl.BlockSpec((1, HEAD_DIM, BLOCK), lambda bi, g, n: (bi, vblk + g, f(n)))
    same = lambda n: n
    return pl.pallas_call(
        functools.partial(_window_body, n_blocks=nb),
        grid=(b, 2, nb),
        in_specs=[
            pl.BlockSpec((1, 256, BLOCK), lambda bi, g, n: (bi, qblk + g, n)),
            kspec(prev), kspec(same), kspec(nxt),
            vspec(prev), vspec(same), vspec(nxt),
            pl.BlockSpec((1, 3 * BLOCK, 4 * BLOCK), lambda bi, g, n: (g, 0, 0)),
            pl.BlockSpec((1, 1, 4 * BLOCK), lambda bi, g, n: (g, 0, 0)),
        ],
        out_specs=pl.BlockSpec((1, BLOCK, 256), lambda bi, g, n: (bi, n, g)),
        out_shape=jax.ShapeDtypeStruct((b, s, BRANCH_WIDTH), BF16),
        compiler_params=_params(("parallel", "parallel", "parallel")),
        name="attn_window",
    )(qkv, k_nat, k_nat, k_nat, qkv, qkv, qkv, bias, sink)


def _merge_body(x_ref, g_ref, ya_ref, yb_ref, yc_ref, wgate_ref, wb_ref, wo_ref, o_ref):
    x = x_ref[0]
    h = _rms_rows(x, g_ref[...]).astype(BF16)
    merged = None
    for n, y_ref in enumerate((ya_ref, yb_ref, yc_ref)):
        logits = jnp.dot(h, wgate_ref[n], preferred_element_type=F32)
        gate = 1.0 / (1.0 + jnp.exp(-logits))
        branch = jnp.dot(y_ref[0], wb_ref[n], preferred_element_type=F32)
        term = gate * branch
        merged = term if merged is None else merged + term
    o_ref[0] = x + jnp.dot(merged.astype(BF16), wo_ref[...], preferred_element_type=F32)


def _merge(x, gain, ya, yb, yc, w_gate, w_branch, w_out, *, tm):
    b, s, d = x.shape
    const = _const_spec
    yspec =pl.BlockSpec((1, tm, BRANCH_WIDTH), lambda bi, i: (bi, i, 0))
    return pl.pallas_call(
        _merge_body,
        grid=(b, s // tm),
        in_specs=[
            pl.BlockSpec((1, tm, d), lambda bi, i: (bi, i, 0)),
            const((1, d)), yspec, yspec, yspec,
            const((N_BRANCH, d, d)), const((N_BRANCH, BRANCH_WIDTH, d)), const((d, d)),
        ],
        out_specs=pl.BlockSpec((1, tm, d), lambda bi, i: (bi, i, 0)),
        out_shape=jax.ShapeDtypeStruct((b, s, d), F32),
        compiler_params=_params(("parallel", "parallel")),
        name="merge",
    )(x, gain.reshape(1, d), ya, yb, yc, w_gate, w_branch, w_out)


def _t5_bucket(rel):
    nb = NUM_BUCKETS // 2
    max_exact = nb // 2
    side = jnp.where(rel > 0, nb, 0)
    n = jnp.abs(rel)
    nf = jnp.maximum(n, 1).astype(F32)
    large = max_exact + (jnp.log(nf / max_exact) / math.log(MAX_DISTANCE / max_exact) * (nb - max_exact)).astype(jnp.int32)
    large = jnp.minimum(large, nb - 1)
    return side + jnp.where(n < max_exact, n, large)


def _rope_table(s):
    rows = s // GRID_W
    row_ids = jnp.repeat(jnp.arange(rows), GRID_W).astype(F32)
    col_ids = jnp.tile(jnp.arange(GRID_W), rows).astype(F32)
    half = HEAD_DIM // 2
    freqs = ROPE_THETA ** (-jnp.arange(0, half, 2, dtype=F32) / half)
    ang_r = row_ids[:, None] * freqs
    ang_c = col_ids[:, None] * freqs
    return jnp.concatenate([jnp.cos(ang_r), jnp.sin(ang_r), jnp.cos(ang_c), jnp.sin(ang_c)], axis=1).T


def _window_bias(rel_bias):
    k = jnp.arange(3 * BLOCK)[:, None]
    q = jnp.arange(BLOCK)[None, :]
    rel = k - BLOCK - q
    tab = rel_bias[:, :B_HEADS][_t5_bucket(rel)].astype(F32) * LOG2E
    tab = jnp.where((jnp.abs(rel) <= WINDOW)[:, :, None], tab, NEG_BIG)
    tab = tab.reshape(3 * BLOCK, BLOCK, B_KV, B_HEADS // B_KV).transpose(2, 0, 3, 1)
    return tab.reshape(B_KV, 3 * BLOCK, (B_HEADS // B_KV) * BLOCK)


def _diff_bias(rel_bias, tq, tk):
    lo, hi, g = _near_offsets(tq, tk)
    table = rel_bias[:, B_HEADS:].astype(F32) * LOG2E
    d = jnp.arange(lo, hi + 1, g)[:, None, None]
    rel = d + jnp.arange(tk)[None, :, None] - jnp.arange(tq)[None, None, :]
    tiles = table[_t5_bucket(rel)].transpose(3, 0, 1, 2)
    nb = NUM_BUCKETS // 2
    cfar = jnp.stack([table[nb - 1], table[NUM_BUCKETS - 1]], axis=1)
    return tiles, cfar


def _natural_keys(qkv, row0, lead):
    b, _, s = qkv.shape
    n = int(np.prod(lead))
    kt = qkv[:, row0:row0 + n * HEAD_DIM, :].reshape((b,) + tuple(lead) + (HEAD_DIM, s))
    return jnp.swapaxes(kt, -1, -2)


def kernel(x, rel_bias, norm_ffn1, w_ffn1_in, w_ffn1_out, norm_mix, w_in, qnorm_a, knorm_a, sink_b, lam_q1, lam_k1, lam_q2, lam_k2, subln_c, w_branch, w_out, norm_ffn2, w_ffn2_in, w_ffn2_out, norm_final):
    b, s, d = x.shape
    depth = w_in.shape[0]
    tm = min(512, s)
    tq = min(256, s)
    tk = min(512, s)

    rope_tab = _rope_table(s)
    win_bias = _window_bias(rel_bias)
    diff_tiles, diff_far = _diff_bias(rel_bias, tq, tk)

    for l in range(depth):
        lambda_init = 0.8 - 0.6 * math.exp(-0.3 * l)
        x = _ffn(x, norm_ffn1[l], w_ffn1_in[l], w_ffn1_out[l], tm=tm)

        w_qkv_t = w_in[l][:, :QKV_ROWS].T.astype(BF16)
        w_gate = w_in[l][:, QKV_ROWS:].reshape(d, N_BRANCH, d).transpose(1, 0, 2).astype(BF16)
        qkv = _inproj(x, norm_mix[l], w_qkv_t, qnorm_a[l], knorm_a[l], rope_tab, tm=tm)

        ka = _natural_keys(qkv, KA0, (A_KV,))
        kb = _natural_keys(qkv, KB0, (B_KV,))
        kc = _natural_keys(qkv, KC0, (C_KV, 2))

        ya = _dense_attn(qkv, ka, q_row0=QA0, v_row0=VA0, dv=HEAD_DIM, tq=tq, tk=tk)
        sink = jnp.repeat(sink_b[l].astype(F32).reshape(B_KV, 1, B_HEADS // B_KV) * LOG2E, BLOCK, axis=2)
        yb = _window_attn(qkv, kb, win_bias, sink)
        lam_vecs = jnp.stack([lam_q1[l], lam_k1[l], lam_q2[l], lam_k2[l]]).astype(F32)
        yc = _dense_attn(qkv, kc, q_row0=QC0, v_row0=VC0, dv=C_V_DIM, tq=tq, tk=tk,
                         diff_args=(diff_tiles, diff_far, lam_vecs, subln_c[l], lambda_init))

        x = _merge(x, norm_mix[l], ya, yb, yc, w_gate, w_branch[l].astype(BF16), w_out[l].astype(BF16), tm=tm)
        x = _ffn(x, norm_ffn2[l], w_ffn2_in[l], w_ffn2_out[l],
                 final_gain=norm_final if l == depth - 1 else None, tm=tm)
    return x
```

```python
import functools
import math

import jax
import jax.numpy as jnp
import numpy as np
from jax import lax
from jax.experimental import pallas as pl
from jax.experimental.pallas import tpu as pltpu

F32 = jnp.float32
BF16 = jnp.bfloat16

HEAD_DIM = 64
A_HEADS, A_KV = 8, 2
B_HEADS, B_KV = 8, 2
C_HEADS, C_KV = 4, 2
C_V_DIM = 2 * HEAD_DIM
WINDOW = 128
BLOCK = 128
GRID_W = 64
ROPE_THETA = 10000.0
NUM_BUCKETS = 32
MAX_DISTANCE = 128
N_BRANCH = 3
BRANCH_WIDTH = 512
EPS = 1e-6
NEG_BIG = -1e30
LOG2E = math.log2(math.e)
SCALE = HEAD_DIM ** -0.5

QA0, KA0, VA0 = 0, 512, 640
QB0, KB0, VB0 = 768, 1280, 1408
QC0, KC0, VC0 = 1536, 2048, 2304
QKV_ROWS = 2560

ONES_ROWS = 16
VMEM_LIMIT_BYTES = 56 * 1024 * 1024


def _params(semantics):
    return pltpu.CompilerParams(dimension_semantics=semantics, vmem_limit_bytes=VMEM_LIMIT_BYTES)


def _const_spec(shape):
    return pl.BlockSpec(shape, lambda *_: (0,) * len(shape), pipeline_mode=pl.Buffered(1))


def _rms_rows(x, gain):
    r = lax.rsqrt(jnp.mean(x * x, axis=-1, keepdims=True) + EPS)
    return x * r * gain


def _ffn_body(x_ref, g_ref, wg_ref, wu_ref, wo_ref, gf_ref, o_ref, *, chunks, final_norm):
    x = x_ref[0]
    h = _rms_rows(x, g_ref[...]).astype(BF16)
    acc = None
    for c0, cw in chunks:
        gg = jnp.dot(h, wg_ref[:, c0:c0 + cw], preferred_element_type=F32)
        uu = jnp.dot(h, wu_ref[:, c0:c0 + cw], preferred_element_type=F32)
        act = (gg * (1.0 / (1.0 + jnp.exp(-gg))) * uu).astype(BF16)
        part = jnp.dot(act, wo_ref[c0:c0 + cw, :], preferred_element_type=F32)
        acc = part if acc is None else acc + part
    y = x + 0.5 * acc
    if final_norm:
        y = _rms_rows(y, gf_ref[...])
    o_ref[0] = y


def _ffn(x, gain, w_in, w_out, final_gain=None, *, tm):
    b, s, d = x.shape
    f = w_out.shape[0]
    wg = w_in[:, :f].astype(BF16)
    wu = w_in[:, f:].astype(BF16)
    wo = w_out.astype(BF16)
    half = f // 2
    chunks = ((0, half), (half, f - half)) if half % 128 == 0 else ((0, f),)
    final_norm = final_gain is not None
    gf = (final_gain if final_norm else gain).reshape(1, d)
    const = _const_spec
    return pl.pallas_call(
        functools.partial(_ffn_body, chunks=chunks, final_norm=final_norm),
        grid=(b, s // tm),
        in_specs=[
            pl.BlockSpec((1, tm, d), lambda bi, i: (bi, i, 0)),
            const((1, d)), const((d, f)), const((d, f)), const((f, d)), const((1, d)),
        ],
        out_specs=pl.BlockSpec((1, tm, d), lambda bi, i: (bi, i, 0)),
        out_shape=jax.ShapeDtypeStruct((b, s, d), F32),
        compiler_params=_params(("parallel", "parallel")),
        name="ffn",
    )(x, gain.reshape(1, d), wg, wu, wo, gf)


def _rope_rows(x, tab):
    cr, sr, cc, sc = tab[0:16], tab[16:32], tab[32:48], tab[48:64]
    x1r, x2r, x1c, x2c = x[0:16], x[16:32], x[32:48], x[48:64]
    return jnp.concatenate(
        [x1r * cr - x2r * sr, x1r * sr + x2r * cr, x1c * cc - x2c * sc, x1c * sc + x2c * cc], axis=0)


def _head_norm_rope(x, gain, tab):
    r = lax.rsqrt(jnp.mean(x * x, axis=0, keepdims=True) + EPS)
    return _rope_rows(x * r * gain, tab)


def _inproj_body(x_ref, g_ref, wt_ref, qg_ref, kg_ref, tab_ref, o_ref):
    h = _rms_rows(x_ref[0], g_ref[...]).astype(BF16)
    yt = lax.dot_general(wt_ref[...], h, (((1,), (1,)), ((), ())),
                         preferred_element_type=F32)
    tab = tab_ref[...]
    qscale = SCALE * LOG2E
    for hd in range(A_HEADS):
        r0 = QA0 + hd * HEAD_DIM
        q = _head_norm_rope(yt[r0:r0 + HEAD_DIM], qg_ref[...], tab)
        o_ref[0, r0:r0 + HEAD_DIM, :] = (q * qscale).astype(BF16)
    for hd in range(A_KV):
        r0 = KA0 + hd * HEAD_DIM
        o_ref[0, r0:r0 + HEAD_DIM, :] = _head_norm_rope(yt[r0:r0 + HEAD_DIM], kg_ref[...], tab).astype(BF16)
    o_ref[0, VA0:QB0, :] = yt[VA0:QB0].astype(BF16)
    o_ref[0, QB0:KB0, :] = (yt[QB0:KB0] * qscale).astype(BF16)
    o_ref[0, KB0:QC0, :] = yt[KB0:QC0].astype(BF16)
    o_ref[0, QC0:KC0, :] = (yt[QC0:KC0] * qscale).astype(BF16)
    o_ref[0, KC0:QKV_ROWS, :] = yt[KC0:QKV_ROWS].astype(BF16)


def _inproj(x, gain, w_qkv_t, qgain, kgain, rope_tab, *, tm):
    b, s, d = x.shape
    const = _const_spec
    return pl.pallas_call(
        _inproj_body,
        grid=(b, s // tm),
        in_specs=[
            pl.BlockSpec((1, tm, d), lambda bi, i: (bi, i, 0)),
            const((1, d)), const((QKV_ROWS, d)), const((HEAD_DIM, tm)), const((HEAD_DIM, tm)),
            pl.BlockSpec((HEAD_DIM, tm), lambda bi, i: (0, i)),
        ],
        out_specs=pl.BlockSpec((1, QKV_ROWS, tm), lambda bi, i: (bi, 0, i)),
        out_shape=jax.ShapeDtypeStruct((b, QKV_ROWS, s), BF16),
        compiler_params=_params(("parallel", "parallel")),
        name="inproj",
    )(x, gain.reshape(1, d), w_qkv_t,
      jnp.broadcast_to(qgain.reshape(HEAD_DIM, 1), (HEAD_DIM, tm)),
      jnp.broadcast_to(kgain.reshape(HEAD_DIM, 1), (HEAD_DIM, tm)),
      rope_tab)


def _dense_attn_body(*refs, dv, tq, tk, n_k, unroll, diff, near_lo, near_step, n_near, near_shift, lambda_init):
    near_pad = near_shift * near_step
    if diff:
        (q_ref, k_ref, v_ref, bias_ref, lam_ref, sg_ref, o_ref,
         acc_ref, m_ref, s_ref, off_ref, alpha_ref) = refs
    else:
        (q_ref, k_ref, v_ref, o_ref, acc_ref, m_ref, s_ref, off_ref, alpha_ref) = refs
    n_maps = 4
    qi = pl.program_id(2)
    ones = jnp.ones((ONES_ROWS, tk), BF16)

    acc_ref[...] = jnp.zeros(acc_ref.shape, F32)
    m_ref[...] = jnp.full(m_ref.shape, NEG_BIG, F32)
    s_ref[1] = jnp.zeros(s_ref.shape[1:], F32)
    off_ref[1] = jnp.full(off_ref.shape[1:], -NEG_BIG, F32)
    alpha_ref[1] = jnp.ones(alpha_ref.shape[1:], F32)

    def score(j, par):
        ks = pl.multiple_of(j * tk, tk)
        if diff:
            di = jnp.clip(lax.div(j * tk - qi * tq - near_lo + near_pad, near_step) - near_shift + 1, 0, n_near + 1)
        for mp in range(n_maps):
            if diff:
                kmat = k_ref[0, 0, mp % 2, pl.ds(ks, tk), :]
            else:
                kmat = k_ref[0, 0, pl.ds(ks, tk), :]
            s = jnp.dot(kmat, q_ref[0, mp * HEAD_DIM:(mp + 1) * HEAD_DIM, :],
                        preferred_element_type=F32)
            if diff:
                s = s + bias_ref[mp // 2, di]
            m_old = m_ref[mp]
            m_new = jnp.maximum(m_old, jnp.max(s, axis=0, keepdims=True))
            m_ref[mp] = m_new
            alpha_ref[par, mp] = jnp.exp2(m_old - m_new)
            off_ref[par, mp] = m_new
            s_ref[par, mp] = s

    def accumulate(j, par):
        ks = max(j, 0) * tk if isinstance(j, int) else pl.multiple_of(jnp.maximum(j, 0) * tk, tk)
        v_aug = jnp.concatenate([v_ref[0, :, pl.ds(ks, tk)], ones], axis=0)
        for mp in range(n_maps):
            p = jnp.exp2(s_ref[par, mp] - off_ref[par, mp]).astype(BF16)
            pv = jnp.dot(v_aug, p, preferred_element_type=F32)
            acc_ref[mp] = alpha_ref[par, mp] * acc_ref[mp] + pv

    def body(i, carry):
        for u in range(unroll):
            j = unroll * i + u
            score(j, u % 2)
            accumulate(j - 1, 1 - u % 2)
        return carry

    lax.fori_loop(0, n_k // unroll, body, 0)
    accumulate(n_k - 1, 1)

    if diff:
        lam = (jnp.exp(jnp.sum(lam_ref[0:1] * lam_ref[1:2], axis=-1, keepdims=True))
               - jnp.exp(jnp.sum(lam_ref[2:3] * lam_ref[3:4], axis=-1, keepdims=True)) + lambda_init)
        outs = []
        for hh in range(2):
            a0 = acc_ref[2 * hh]
            a1 = acc_ref[2 * hh + 1]
            o = a0[0:dv] / a0[dv:dv + 1] - lam * (a1[0:dv] / a1[dv:dv + 1])
            r = lax.rsqrt(jnp.mean(o * o, axis=0, keepdims=True) + EPS)
            outs.append(o * r * sg_ref[...] * (1.0 - lambda_init))
        ot = jnp.concatenate(outs, axis=0)
    else:
        outs = []
        for mp in range(n_maps):
            a = acc_ref[mp]
            outs.append(a[0:dv] / a[dv:dv + 1])
        ot = jnp.concatenate(outs, axis=0)
    o_ref[0] = ot.T.astype(BF16)


def _near_offsets(tq, tk):
    g = math.gcd(tq, tk)
    ds = [d for d in range(-(tk // g + 1) * g, (tq // g + 2) * g, g)
          if d - (tq - 1) <= MAX_DISTANCE - 1 and d + (tk - 1) >= -(MAX_DISTANCE - 1)]
    return ds[0], ds[-1], g


def _dense_attn(qkv, k_nat, *, q_row0, v_row0, dv, tq, tk, diff_args=None):
    b, _, s = qkv.shape
    n_k = s // tk
    unroll = 4 if n_k % 4 == 0 else 2
    assert n_k % unroll == 0
    diff = diff_args is not None
    qblk = q_row0 // 256
    vblk = v_row0 // dv
    once = pl.Buffered(1)
    in_specs = [
        pl.BlockSpec((1, 256, tq), lambda bi, g, i: (bi, qblk + g, i)),
        (pl.BlockSpec((1, 1, 2, s, HEAD_DIM), lambda bi, g, i: (bi, g, 0, 0, 0), pipeline_mode=once) if diff
         else pl.BlockSpec((1, 1, s, HEAD_DIM), lambda bi, g, i: (bi, g, 0, 0), pipeline_mode=once)),
        pl.BlockSpec((1, dv, s), lambda bi, g, i: (bi, vblk + g, 0), pipeline_mode=once),
    ]
    args = [qkv, k_nat, qkv]
    kw = dict(near_lo=0, near_step=1, n_near=0, near_shift=0, lambda_init=0.0)
    if diff:
        bias_tiles, lam_vecs, subln, lambda_init = diff_args
        lo, hi, g_ = _near_offsets(tq, tk)
        n_d = (hi - lo) // g_ + 1
        kw = dict(near_lo=lo, near_step=g_, n_near=n_d, near_shift=(s + 2 * tk) // g_ + 1, lambda_init=lambda_init)
        in_specs += [
            pl.BlockSpec((2, n_d + 2, tk, tq), lambda bi, g, i: (g, 0, 0, 0), pipeline_mode=once),
            _const_spec((4, HEAD_DIM)),
            _const_spec((dv, tq)),
        ]
        args += [bias_tiles, lam_vecs, jnp.broadcast_to(subln.reshape(dv, 1), (dv, tq))]
    return pl.pallas_call(
        functools.partial(_dense_attn_body, dv=dv, tq=tq, tk=tk, n_k=n_k, unroll=unroll, diff=diff, **kw),
        grid=(b, 2, s // tq),
        in_specs=in_specs,
        out_specs=pl.BlockSpec((1, tq, 256), lambda bi, g, i: (bi, i, g)),
        out_shape=jax.ShapeDtypeStruct((b, s, BRANCH_WIDTH), BF16),
        scratch_shapes=[
            pltpu.VMEM((4, dv + ONES_ROWS, tq), F32),
            pltpu.VMEM((4, 1, tq), F32),
            pltpu.VMEM((2, 4, tk, tq), F32),
            pltpu.VMEM((2, 4, 1, tq), F32),
            pltpu.VMEM((2, 4, 1, tq), F32),
        ],
        compiler_params=_params(("parallel", "parallel", "parallel")),
        name="attn_diff" if diff else "attn_axial",
    )(*args)


def _window_body(q_ref, kp_ref, kc_ref, kn_ref, vp_ref, vc_ref, vn_ref, bias_ref, sink_ref, o_ref, *, n_blocks):
    n = pl.program_id(2)
    qt = q_ref[0]
    q_all = jnp.concatenate([qt[h * HEAD_DIM:(h + 1) * HEAD_DIM] for h in range(4)], axis=1)
    kband = jnp.concatenate([kp_ref[0, 0], kc_ref[0, 0], kn_ref[0, 0]], axis=0)
    vband = jnp.concatenate([vp_ref[0], vc_ref[0], vn_ref[0]], axis=1)
    s = jnp.dot(kband, q_all, preferred_element_type=F32) + bias_ref[0]
    row = lax.broadcasted_iota(jnp.int32, s.shape, 0)
    valid = jnp.logical_and(jnp.logical_or(n > 0, row >= BLOCK),
                            jnp.logical_or(n < n_blocks - 1, row < 2 * BLOCK))
    s = jnp.where(valid, s, NEG_BIG)
    sk = sink_ref[0]
    m = jnp.maximum(jnp.max(s, axis=0, keepdims=True), sk)
    e = jnp.exp2(s - m)
    denom = jnp.sum(e, axis=0, keepdims=True) + jnp.exp2(sk - m)
    p = (e / denom).astype(BF16)
    outs = [jnp.dot(vband, p[:, h * BLOCK:(h + 1) * BLOCK], preferred_element_type=F32) for h in range(4)]
    o_ref[0] = jnp.concatenate(outs, axis=0).T.astype(BF16)


def _window_attn(qkv, k_nat, bias, sink):
    b, _, s = qkv.shape
    nb = s // BLOCK
    qblk = QB0 // 256
    vblk = VB0 // HEAD_DIM
    prev = lambda n: jnp.maximum(n - 1, 0)
    nxt = lambda n: jnp.minimum(n + 1, nb - 1)
    kspec = lambda f: pl.BlockSpec((1, 1, BLOCK, HEAD_DIM), lambda bi, g, n: (bi, g, f(n), 0))
    vspec = lambda f: pl.BlockSpec((1, HEAD_DIM, BLOCK), lambda bi, g, n: (bi, vblk + g, f(n)))
    same = lambda n: n
    return pl.pallas_call(
        functools.partial(_window_body, n_blocks=nb),
        grid=(b, 2, nb),
        in_specs=[
            pl.BlockSpec((1, 256, BLOCK), lambda bi, g, n: (bi, qblk + g, n)),
            kspec(prev), kspec(same), kspec(nxt),
            vspec(prev), vspec(same), vspec(nxt),
            pl.BlockSpec((1, 3 * BLOCK, 4 * BLOCK), lambda bi, g, n: (g, 0, 0)),
            pl.BlockSpec((1, 1, 4 * BLOCK), lambda bi, g, n: (g, 0, 0)),
        ],
        out_specs=pl.BlockSpec((1, BLOCK, 256), lambda bi, g, n: (bi, n, g)),
        out_shape=jax.ShapeDtypeStruct((b, s, BRANCH_WIDTH), BF16),
        compiler_params=_params(("parallel", "parallel", "parallel")),
        name="attn_window",
    )(qkv, k_nat, k_nat, k_nat, qkv, qkv, qkv, bias, sink)


def _merge_body(x_ref, g_ref, ya_ref, yb_ref, yc_ref, wgate_ref, wb_ref, wo_ref, o_ref):
    x = x_ref[0]
    h = _rms_rows(x, g_ref[...]).astype(BF16)
    merged = None
    for n, y_ref in enumerate((ya_ref, yb_ref, yc_ref)):
        logits = jnp.dot(h, wgate_ref[n], preferred_element_type=F32)
        gate = 1.0 / (1.0 + jnp.exp(-logits))
        branch = jnp.dot(y_ref[0], wb_ref[n], preferred_element_type=F32)
        term = gate * branch
        merged = term if merged is None else merged + term
    o_ref[0] = x + jnp.dot(merged.astype(BF16), wo_ref[...], preferred_element_type=F32)


def _merge(x, gain, ya, yb, yc, w_gate, w_branch, w_out, *, tm):
    b, s, d = x.shape
    const = _const_spec
    yspec =pl.BlockSpec((1, tm, BRANCH_WIDTH), lambda bi, i: (bi, i, 0))
    return pl.pallas_call(
        _merge_body,
        grid=(b, s // tm),
        in_specs=[
            pl.BlockSpec((1, tm, d), lambda bi, i: (bi, i, 0)),
            const((1, d)), yspec, yspec, yspec,
            const((N_BRANCH, d, d)), const((N_BRANCH, BRANCH_WIDTH, d)), const((d, d)),
        ],
        out_specs=pl.BlockSpec((1, tm, d), lambda bi, i: (bi, i, 0)),
        out_shape=jax.ShapeDtypeStruct((b, s, d), F32),
        compiler_params=_params(("parallel", "parallel")),
        name="merge",
    )(x, gain.reshape(1, d), ya, yb, yc, w_gate, w_branch, w_out)


def _t5_bucket(rel):
    nb = NUM_BUCKETS // 2
    max_exact = nb // 2
    side = jnp.where(rel > 0, nb, 0)
    n = jnp.abs(rel)
    nf = jnp.maximum(n, 1).astype(F32)
    large = max_exact + (jnp.log(nf / max_exact) / math.log(MAX_DISTANCE / max_exact) * (nb - max_exact)).astype(jnp.int32)
    large = jnp.minimum(large, nb - 1)
    return side + jnp.where(n < max_exact, n, large)


def _bucket_lookup(table, bucket):
    expand = (slice(None),) + (None,) * bucket.ndim
    out = jnp.zeros((table.shape[1],) + bucket.shape, F32)
    for bk in range(NUM_BUCKETS):
        out = jnp.where(bucket[None] == bk, table[bk][expand], out)
    return out


def _rope_table(s):
    rows = s // GRID_W
    row_ids = jnp.repeat(jnp.arange(rows), GRID_W).astype(F32)
    col_ids = jnp.tile(jnp.arange(GRID_W), rows).astype(F32)
    half = HEAD_DIM // 2
    freqs = ROPE_THETA ** (-jnp.arange(0, half, 2, dtype=F32) / half)
    ang_r = row_ids[:, None] * freqs
    ang_c = col_ids[:, None] * freqs
    return jnp.concatenate([jnp.cos(ang_r), jnp.sin(ang_r), jnp.cos(ang_c), jnp.sin(ang_c)], axis=1).T


def _window_bias(rel_bias):
    k = jnp.arange(3 * BLOCK)[:, None]
    q = jnp.arange(BLOCK)[None, :]
    rel = k - BLOCK - q
    tab = _bucket_lookup(rel_bias[:, :B_HEADS].astype(F32) * LOG2E, _t5_bucket(rel))
    tab = jnp.where((jnp.abs(rel) <= WINDOW)[None], tab, NEG_BIG)
    tab = tab.reshape(B_KV, B_HEADS // B_KV, 3 * BLOCK, BLOCK).transpose(0, 2, 1, 3)
    return tab.reshape(B_KV, 3 * BLOCK, (B_HEADS // B_KV) * BLOCK)


def _diff_bias(rel_bias, tq, tk):
    lo, hi, g = _near_offsets(tq, tk)
    table = rel_bias[:, B_HEADS:].astype(F32) * LOG2E
    d = jnp.arange(lo - g, hi + g + 1, g)[:, None, None]
    rel = d + jnp.arange(tk)[None, :, None] - jnp.arange(tq)[None, None, :]
    return _bucket_lookup(table, _t5_bucket(rel))


def _natural_keys(qkv, row0, lead):
    b, _, s = qkv.shape
    n = int(np.prod(lead))
    kt = qkv[:, row0:row0 + n * HEAD_DIM, :].reshape((b,) + tuple(lead) + (HEAD_DIM, s))
    return jnp.swapaxes(kt, -1, -2)


def kernel(x, rel_bias, norm_ffn1, w_ffn1_in, w_ffn1_out, norm_mix, w_in, qnorm_a, knorm_a, sink_b, lam_q1, lam_k1, lam_q2, lam_k2, subln_c, w_branch, w_out, norm_ffn2, w_ffn2_in, w_ffn2_out, norm_final):
    b, s, d = x.shape
    depth = w_in.shape[0]
    tm = min(512, s)
    tq = min(256, s)
    tk = min(512, s)

    rope_tab = _rope_table(s)
    win_bias = _window_bias(rel_bias)
    diff_tiles = _diff_bias(rel_bias, tq, tk)

    for l in range(depth):
        lambda_init = 0.8 - 0.6 * math.exp(-0.3 * l)
        x = _ffn(x, norm_ffn1[l], w_ffn1_in[l], w_ffn1_out[l], tm=tm)

        w_qkv_t = w_in[l][:, :QKV_ROWS].T.astype(BF16)
        w_gate = w_in[l][:, QKV_ROWS:].reshape(d, N_BRANCH, d).transpose(1, 0, 2).astype(BF16)
        qkv = _inproj(x, norm_mix[l], w_qkv_t, qnorm_a[l], knorm_a[l], rope_tab, tm=tm)

        ka = _natural_keys(qkv, KA0, (A_KV,))
        kb = _natural_keys(qkv, KB0, (B_KV,))
        kc = _natural_keys(qkv, KC0, (C_KV, 2))

        ya = _dense_attn(qkv, ka, q_row0=QA0, v_row0=VA0, dv=HEAD_DIM, tq=tq, tk=tk)
        sink = jnp.repeat(sink_b[l].astype(F32).reshape(B_KV, 1, B_HEADS // B_KV) * LOG2E, BLOCK, axis=2)
        yb = _window_attn(qkv, kb, win_bias, sink)
        lam_vecs = jnp.stack([lam_q1[l], lam_k1[l], lam_q2[l], lam_k2[l]]).astype(F32)
        yc = _dense_attn(qkv, kc, q_row0=QC0, v_row0=VC0, dv=C_V_DIM, tq=tq, tk=tk,
                         diff_args=(diff_tiles, lam_vecs, subln_c[l], lambda_init))

        x = _merge(x, norm_mix[l], ya, yb, yc, w_gate, w_branch[l].astype(BF16), w_out[l].astype(BF16), tm=tm)
        x = _ffn(x, norm_ffn2[l], w_ffn2_in[l], w_ffn2_out[l],
                 final_gain=norm_final if l == depth - 1 else None, tm=tm)
    return x
```

```python
import functools
import math

import jax
import jax.numpy as jnp
import numpy as np
from jax import lax
from jax.experimental import pallas as pl
from jax.experimental.pallas import tpu as pltpu

F32 = jnp.float32
BF16 = jnp.bfloat16

HEAD_DIM = 64
A_HEADS, A_KV = 8, 2
B_HEADS, B_KV = 8, 2
C_HEADS, C_KV = 4, 2
C_V_DIM = 2 * HEAD_DIM
WINDOW = 128
BLOCK = 128
GRID_W = 64
ROPE_THETA = 10000.0
NUM_BUCKETS = 32
MAX_DISTANCE = 128
N_BRANCH = 3
BRANCH_WIDTH = 512
EPS = 1e-6
NEG_BIG = -1e30
LOG2E = math.log2(math.e)
SCALE = HEAD_DIM ** -0.5

QA0, KA0, VA0 = 0, 512, 640
QB0, KB0, VB0 = 768, 1280, 1408
QC0, KC0, VC0 = 1536, 2048, 2304
QKV_ROWS = 2560

SKEW = 2
SAFE_BOUND = 40.0
BOUND_SLACK = 1.0 + 2.0 ** -10
ONES_ROWS = 16
VMEM_LIMIT_BYTES = 56 * 1024 * 1024


def _params(semantics):
    return pltpu.CompilerParams(dimension_semantics=semantics, vmem_limit_bytes=VMEM_LIMIT_BYTES)


def _const_spec(shape):
    return pl.BlockSpec(shape, lambda *_: (0,) * len(shape), pipeline_mode=pl.Buffered(1))


def _rms_rows(x, gain):
    r = lax.rsqrt(jnp.mean(x * x, axis=-1, keepdims=True) + EPS)
    return x * r * gain


def _ffn_body(x_ref, g_ref, wg_ref, wu_ref, wo_ref, gf_ref, o_ref, *, chunks, final_norm):
    x = x_ref[0]
    h = _rms_rows(x, g_ref[...]).astype(BF16)
    acc = None
    for c0, cw in chunks:
        gg = jnp.dot(h, wg_ref[:, c0:c0 + cw], preferred_element_type=F32)
        uu = jnp.dot(h, wu_ref[:, c0:c0 + cw], preferred_element_type=F32)
        act = (gg * (1.0 / (1.0 + jnp.exp(-gg))) * uu).astype(BF16)
        part = jnp.dot(act, wo_ref[c0:c0 + cw, :], preferred_element_type=F32)
        acc = part if acc is None else acc + part
    y = x + 0.5 * acc
    if final_norm:
        y = _rms_rows(y, gf_ref[...])
    o_ref[0] = y


def _ffn(x, gain, w_in, w_out, final_gain=None, *, tm):
    b, s, d = x.shape
    f = w_out.shape[0]
    wg = w_in[:, :f].astype(BF16)
    wu = w_in[:, f:].astype(BF16)
    wo = w_out.astype(BF16)
    half = f // 2
    chunks = ((0, half), (half, f - half)) if half % 128 == 0 else ((0, f),)
    final_norm = final_gain is not None
    gf = (final_gain if final_norm else gain).reshape(1, d)
    const = _const_spec
    return pl.pallas_call(
        functools.partial(_ffn_body, chunks=chunks, final_norm=final_norm),
        grid=(b, s // tm),
        in_specs=[
            pl.BlockSpec((1, tm, d), lambda bi, i: (bi, i, 0)),
            const((1, d)), const((d, f)), const((d, f)), const((f, d)), const((1, d)),
        ],
        out_specs=pl.BlockSpec((1, tm, d), lambda bi, i: (bi, i, 0)),
        out_shape=jax.ShapeDtypeStruct((b, s, d), F32),
        compiler_params=_params(("parallel", "parallel")),
        name="ffn",
    )(x, gain.reshape(1, d), wg, wu, wo, gf)


def _rope_rows(x, tab):
    cr, sr, cc, sc = tab[0:16], tab[16:32], tab[32:48], tab[48:64]
    x1r, x2r, x1c, x2c = x[0:16], x[16:32], x[32:48], x[48:64]
    return jnp.concatenate(
        [x1r * cr - x2r * sr, x1r * sr + x2r * cr, x1c * cc - x2c * sc, x1c * sc + x2c * cc], axis=0)


def _head_norm_rope(x, gain, tab):
    r = lax.rsqrt(jnp.mean(x * x, axis=0, keepdims=True) + EPS)
    return _rope_rows(x * r * gain, tab)


def _inproj_body(x_ref, g_ref, wt_ref, qg_ref, kg_ref, tab_ref, o_ref):
    h = _rms_rows(x_ref[0], g_ref[...]).astype(BF16)
    yt = lax.dot_general(wt_ref[...], h, (((1,), (1,)), ((), ())),
                         preferred_element_type=F32)
    tab = tab_ref[...]
    qscale = SCALE * LOG2E
    for hd in range(A_HEADS):
        r0 = QA0 + hd * HEAD_DIM
        q = _head_norm_rope(yt[r0:r0 + HEAD_DIM], qg_ref[...], tab)
        o_ref[0, r0:r0 + HEAD_DIM, :] = (q * qscale).astype(BF16)
    for hd in range(A_KV):
        r0 = KA0 + hd * HEAD_DIM
        o_ref[0, r0:r0 + HEAD_DIM, :] = _head_norm_rope(yt[r0:r0 + HEAD_DIM], kg_ref[...], tab).astype(BF16)
    o_ref[0, VA0:QB0, :] = yt[VA0:QB0].astype(BF16)
    o_ref[0, QB0:KB0, :] = (yt[QB0:KB0] * qscale).astype(BF16)
    o_ref[0, KB0:QC0, :] = yt[KB0:QC0].astype(BF16)
    o_ref[0, QC0:KC0, :] = (yt[QC0:KC0] * qscale).astype(BF16)
    o_ref[0, KC0:QKV_ROWS, :] = yt[KC0:QKV_ROWS].astype(BF16)


def _inproj(x, gain, w_qkv_t, qgain, kgain, rope_tab, *, tm):
    b, s, d = x.shape
    const = _const_spec
    return pl.pallas_call(
        _inproj_body,
        grid=(b, s // tm),
        in_specs=[
            pl.BlockSpec((1, tm, d), lambda bi, i: (bi, i, 0)),
            const((1, d)), const((QKV_ROWS, d)), const((HEAD_DIM, tm)), const((HEAD_DIM, tm)),
            pl.BlockSpec((HEAD_DIM, tm), lambda bi, i: (0, i)),
        ],
        out_specs=pl.BlockSpec((1, QKV_ROWS, tm), lambda bi, i: (bi, 0, i)),
        out_shape=jax.ShapeDtypeStruct((b, QKV_ROWS, s), BF16),
        compiler_params=_params(("parallel", "parallel")),
        name="inproj",
    )(x, gain.reshape(1, d), w_qkv_t,
      jnp.broadcast_to(qgain.reshape(HEAD_DIM, 1), (HEAD_DIM, tm)),
      jnp.broadcast_to(kgain.reshape(HEAD_DIM, 1), (HEAD_DIM, tm)),
      rope_tab)


def _dense_attn_body(*refs, dv, tq, tk, n_k, unroll, diff, near_lo, near_step, n_near, near_shift, lambda_init):
    near_pad = near_shift * near_step
    fast_unroll = 8 if n_k % 8 == 0 else unroll
    if diff:
        (q_ref, k_ref, v_ref, bnd_ref, bias_ref, lam_ref, sg_ref, o_ref,
         acc_ref, m_ref, s_ref, off_ref, alpha_ref) = refs
    else:
        (q_ref, k_ref, v_ref, bnd_ref, o_ref, acc_ref, m_ref, s_ref, off_ref, alpha_ref) = refs
    n_maps = 4
    bi = pl.program_id(0)
    kvh = pl.program_id(1)
    qi = pl.program_id(2)
    ones = jnp.ones((ONES_ROWS, tk), BF16)

    def scores(j, mp):
        ks = pl.multiple_of(j * tk, tk)
        if diff:
            kmat = k_ref[0, 0, mp % 2, pl.ds(ks, tk), :]
        else:
            kmat = k_ref[0, 0, pl.ds(ks, tk), :]
        s = jnp.dot(kmat, q_ref[0, mp * HEAD_DIM:(mp + 1) * HEAD_DIM, :], preferred_element_type=F32)
        if diff:
            di = jnp.clip(lax.div(j * tk - qi * tq - near_lo + near_pad, near_step) - near_shift + 1, 0, n_near + 1)
            s = s + bias_ref[mp // 2, di]
        return s

    def values(j):
        ks = max(j, 0) * tk if isinstance(j, int) else pl.multiple_of(jnp.maximum(j, 0) * tk, tk)
        return jnp.concatenate([v_ref[0, :, pl.ds(ks, tk)], ones], axis=0)

    acc_ref[...] = jnp.zeros(acc_ref.shape, F32)

    bound_max = None
    for mp in range(n_maps):
        q = q_ref[0, mp * HEAD_DIM:(mp + 1) * HEAD_DIM, :].astype(F32)
        qn = jnp.sqrt(jnp.sum(q * q, axis=0, keepdims=True))
        if diff:
            kmax = bnd_ref[(bi * 2 + kvh) * 2 + mp % 2]
            extra = bnd_ref[pl.num_programs(0) * 4 + kvh * 2 + mp // 2]
        else:
            kmax = bnd_ref[bi * 2 + kvh]
            extra = 0.0
        bound = qn * (kmax * BOUND_SLACK) + (extra + (BOUND_SLACK - 1.0))
        off_ref[0, mp] = bound
        tile_max = jnp.max(bound)
        bound_max = tile_max if bound_max is None else jnp.maximum(bound_max, tile_max)
    bounded = bound_max <= SAFE_BOUND

    @pl.when(bounded)
    def _():
        def body(i, carry):
            tiles = [(fast_unroll * i + u, mp) for u in range(fast_unroll) for mp in range(n_maps)]
            pending = [scores(*tiles[t]) for t in range(SKEW)]
            for t, (j, mp) in enumerate(tiles):
                if t + SKEW < len(tiles):
                    pending.append(scores(*tiles[t + SKEW]))
                p = jnp.exp2(pending.pop(0) - off_ref[0, mp]).astype(BF16)
                acc_ref[mp] += jnp.dot(values(j), p, preferred_element_type=F32)
            return carry
        lax.fori_loop(0, n_k // fast_unroll, body, 0)

    @pl.when(jnp.logical_not(bounded))
    def _():
        m_ref[...] = jnp.full(m_ref.shape, NEG_BIG, F32)
        s_ref[1] = jnp.zeros(s_ref.shape[1:], F32)
        off_ref[1] = jnp.full(off_ref.shape[1:], -NEG_BIG, F32)
        alpha_ref[1] = jnp.ones(alpha_ref.shape[1:], F32)

        def score(j, par):
            for mp in range(n_maps):
                s = scores(j, mp)
                m_old = m_ref[mp]
                m_new = jnp.maximum(m_old, jnp.max(s, axis=0, keepdims=True))
                m_ref[mp] = m_new
                alpha_ref[par, mp] = jnp.exp2(m_old - m_new)
                off_ref[par, mp] = m_new
                s_ref[par, mp] = s

        def accumulate(j, par):
            v_aug = values(j)
            for mp in range(n_maps):
                p = jnp.exp2(s_ref[par, mp] - off_ref[par, mp]).astype(BF16)
                pv = jnp.dot(v_aug, p, preferred_element_type=F32)
                acc_ref[mp] = alpha_ref[par, mp] * acc_ref[mp] + pv

        def body(i, carry):
            for u in range(unroll):
                j = unroll * i + u
                score(j, u % 2)
                accumulate(j - 1, 1 - u % 2)
            return carry

        lax.fori_loop(0, n_k // unroll, body, 0)
        accumulate(n_k - 1, 1)

    if diff:
        lam = (jnp.exp(jnp.sum(lam_ref[0:1] * lam_ref[1:2], axis=-1, keepdims=True))
               - jnp.exp(jnp.sum(lam_ref[2:3] * lam_ref[3:4], axis=-1, keepdims=True)) + lambda_init)
        outs = []
        for hh in range(2):
            a0 = acc_ref[2 * hh]
            a1 = acc_ref[2 * hh + 1]
            o = a0[0:dv] / a0[dv:dv + 1] - lam * (a1[0:dv] / a1[dv:dv + 1])
            r = lax.rsqrt(jnp.mean(o * o, axis=0, keepdims=True) + EPS)
            outs.append(o * r * sg_ref[...] * (1.0 - lambda_init))
        ot = jnp.concatenate(outs, axis=0)
    else:
        outs = []
        for mp in range(n_maps):
            a = acc_ref[mp]
            outs.append(a[0:dv] / a[dv:dv + 1])
        ot = jnp.concatenate(outs, axis=0)
    o_ref[0] = ot.T.astype(BF16)


def _near_offsets(tq, tk):
    g = math.gcd(tq, tk)
    ds = [d for d in range(-(tk // g + 1) * g, (tq // g + 2) * g, g)
          if d - (tq - 1) <= MAX_DISTANCE - 1 and d + (tk - 1) >= -(MAX_DISTANCE - 1)]
    return ds[0], ds[-1], g


def _dense_attn(qkv, k_nat, *, q_row0, v_row0, dv, tq, tk, diff_args=None):
    b, _, s = qkv.shape
    n_k = s // tk
    unroll = 4 if n_k % 4 == 0 else 2
    assert n_k % unroll == 0
    diff = diff_args is not None
    qblk = q_row0 // 256
    vblk = v_row0 // dv
    once = pl.Buffered(1)
    in_specs = [
        pl.BlockSpec((1, 256, tq), lambda bi, g, i: (bi, qblk + g, i)),
        (pl.BlockSpec((1, 1, 2, s, HEAD_DIM), lambda bi, g, i: (bi, g, 0, 0, 0), pipeline_mode=once) if diff
         else pl.BlockSpec((1, 1, s, HEAD_DIM), lambda bi, g, i: (bi, g, 0, 0), pipeline_mode=once)),
        pl.BlockSpec((1, dv, s), lambda bi, g, i: (bi, vblk + g, 0), pipeline_mode=once),
    ]
    kmax = jnp.sqrt(jnp.max(jnp.sum(jnp.square(k_nat.astype(F32)), axis=-1), axis=-1)).reshape(-1)
    in_specs.append(pl.BlockSpec(memory_space=pltpu.SMEM))
    args = [qkv, k_nat, qkv, kmax]
    kw = dict(near_lo=0, near_step=1, n_near=0, near_shift=0, lambda_init=0.0)
    if diff:
        bias_tiles, lam_vecs, subln, lambda_init = diff_args
        args[3] = jnp.concatenate([kmax, jnp.max(jnp.abs(bias_tiles), axis=(1, 2, 3))])
        lo, hi, g_ = _near_offsets(tq, tk)
        n_d = (hi - lo) // g_ + 1
        kw = dict(near_lo=lo, near_step=g_, n_near=n_d, near_shift=(s + 2 * tk) // g_ + 1, lambda_init=lambda_init)
        in_specs += [
            pl.BlockSpec((2, n_d + 2, tk, tq), lambda bi, g, i: (g, 0, 0, 0), pipeline_mode=once),
            _const_spec((4, HEAD_DIM)),
            _const_spec((dv, tq)),
        ]
        args += [bias_tiles, lam_vecs, jnp.broadcast_to(subln.reshape(dv, 1), (dv, tq))]
    return pl.pallas_call(
        functools.partial(_dense_attn_body, dv=dv, tq=tq, tk=tk, n_k=n_k, unroll=unroll, diff=diff, **kw),
        grid=(b, 2, s // tq),
        in_specs=in_specs,
        out_specs=pl.BlockSpec((1, tq, 256), lambda bi, g, i: (bi, i, g)),
        out_shape=jax.ShapeDtypeStruct((b, s, BRANCH_WIDTH), BF16),
        scratch_shapes=[
            pltpu.VMEM((4, dv + ONES_ROWS, tq), F32),
            pltpu.VMEM((4, 1, tq), F32),
            pltpu.VMEM((2, 4, tk, tq), F32),
            pltpu.VMEM((2, 4, 1, tq), F32),
            pltpu.VMEM((2, 4, 1, tq), F32),
        ],
        compiler_params=_params(("parallel", "parallel", "parallel")),
        name="attn_diff" if diff else "attn_axial",
    )(*args)


def _window_body(q_ref, kp_ref, kc_ref, kn_ref, vp_ref, vc_ref, vn_ref, bias_ref, sink_ref, o_ref, *, n_blocks):
    n = pl.program_id(2)
    qt = q_ref[0]
    q_all = jnp.concatenate([qt[h * HEAD_DIM:(h + 1) * HEAD_DIM] for h in range(4)], axis=1)
    kband = jnp.concatenate([kp_ref[0, 0], kc_ref[0, 0], kn_ref[0, 0]], axis=0)
    vband = jnp.concatenate([vp_ref[0], vc_ref[0], vn_ref[0]], axis=1)
    s = jnp.dot(kband, q_all, preferred_element_type=F32) + bias_ref[0]
    row = lax.broadcasted_iota(jnp.int32, s.shape, 0)
    valid = jnp.logical_and(jnp.logical_or(n > 0, row >= BLOCK),
                            jnp.logical_or(n < n_blocks - 1, row < 2 * BLOCK))
    s = jnp.where(valid, s, NEG_BIG)
    sk = sink_ref[0]
    m = jnp.maximum(jnp.max(s, axis=0, keepdims=True), sk)
    e = jnp.exp2(s - m)
    denom = jnp.sum(e, axis=0, keepdims=True) + jnp.exp2(sk - m)
    p = (e / denom).astype(BF16)
    outs = [jnp.dot(vband, p[:, h * BLOCK:(h + 1) * BLOCK], preferred_element_type=F32) for h in range(4)]
    o_ref[0] = jnp.concatenate(outs, axis=0).T.astype(BF16)


def _window_attn(qkv, k_nat, bias, sink):
    b, _, s = qkv.shape
    nb = s // BLOCK
    qblk = QB0 // 256
    vblk = VB0 // HEAD_DIM
    prev = lambda n: jnp.maximum(n - 1, 0)
    nxt = lambda n: jnp.minimum(n + 1, nb - 1)
    kspec = lambda f: pl.BlockSpec((1, 1, BLOCK, HEAD_DIM), lambda bi, g, n: (bi, g, f(n), 0))
    vspec = lambda f: pl.BlockSpec((1, HEAD_DIM, BLOCK), lambda bi, g, n: (bi, vblk + g, f(n)))
    same = lambda n: n
    return pl.pallas_call(
        functools.partial(_window_body, n_blocks=nb),
        grid=(b, 2, nb),
        in_specs=[
            pl.BlockSpec((1, 256, BLOCK), lambda bi, g, n: (bi, qblk + g, n)),
            kspec(prev), kspec(same), kspec(nxt),
            vspec(prev), vspec(same), vspec(nxt),
            pl.BlockSpec((1, 3 * BLOCK, 4 * BLOCK), lambda bi, g, n: (g, 0, 0)),
            pl.BlockSpec((1, 1, 4 * BLOCK), lambda bi, g, n: (g, 0, 0)),
        ],
        out_specs=pl.BlockSpec((1, BLOCK, 256), lambda bi, g, n: (bi, n, g)),
        out_shape=jax.ShapeDtypeStruct((b, s, BRANCH_WIDTH), BF16),
        compiler_params=_params(("parallel", "parallel", "parallel")),
        name="attn_window",
    )(qkv, k_nat, k_nat, k_nat, qkv, qkv, qkv, bias, sink)


def _merge_body(x_ref, g_ref, ya_ref, yb_ref, yc_ref, wgate_ref, wb_ref, wo_ref, o_ref):
    x = x_ref[0]
    h = _rms_rows(x, g_ref[...]).astype(BF16)
    merged = None
    for n, y_ref in enumerate((ya_ref, yb_ref, yc_ref)):
        logits = jnp.dot(h, wgate_ref[n], preferred_element_type=F32)
        gate = 1.0 / (1.0 + jnp.exp(-logits))
        branch = jnp.dot(y_ref[0], wb_ref[n], preferred_element_type=F32)
        term = gate * branch
        merged = term if merged is None else merged + term
    o_ref[0] = x + jnp.dot(merged.astype(BF16), wo_ref[...], preferred_element_type=F32)


def _merge(x, gain, ya, yb, yc, w_gate, w_branch, w_out, *, tm):
    b, s, d = x.shape
    const = _const_spec
    yspec =pl.BlockSpec((1, tm, BRANCH_WIDTH), lambda bi, i: (bi, i, 0))
    return pl.pallas_call(
        _merge_body,
        grid=(b, s // tm),
        in_specs=[
            pl.BlockSpec((1, tm, d), lambda bi, i: (bi, i, 0)),
            const((1, d)), yspec, yspec, yspec,
            const((N_BRANCH, d, d)), const((N_BRANCH, BRANCH_WIDTH, d)), const((d, d)),
        ],
        out_specs=pl.BlockSpec((1, tm, d), lambda bi, i: (bi, i, 0)),
        out_shape=jax.ShapeDtypeStruct((b, s, d), F32),
        compiler_params=_params(("parallel", "parallel")),
        name="merge",
    )(x, gain.reshape(1, d), ya, yb, yc, w_gate, w_branch, w_out)


def _t5_bucket(rel):
    nb = NUM_BUCKETS // 2
    max_exact = nb // 2
    side = jnp.where(rel > 0, nb, 0)
    n = jnp.abs(rel)
    nf = jnp.maximum(n, 1).astype(F32)
    large = max_exact + (jnp.log(nf / max_exact) / math.log(MAX_DISTANCE / max_exact) * (nb - max_exact)).astype(jnp.int32)
    large = jnp.minimum(large, nb - 1)
    return side + jnp.where(n < max_exact, n, large)


def _bucket_lookup(table, bucket):
    expand = (slice(None),) + (None,) * bucket.ndim
    out = jnp.zeros((table.shape[1],) + bucket.shape, F32)
    for bk in range(NUM_BUCKETS):
        out = jnp.where(bucket[None] == bk, table[bk][expand], out)
    return out


def _rope_table(s):
    rows = s // GRID_W
    row_ids = jnp.repeat(jnp.arange(rows), GRID_W).astype(F32)
    col_ids = jnp.tile(jnp.arange(GRID_W), rows).astype(F32)
    half = HEAD_DIM // 2
    freqs = ROPE_THETA ** (-jnp.arange(0, half, 2, dtype=F32) / half)
    ang_r = row_ids[:, None] * freqs
    ang_c = col_ids[:, None] * freqs
    return jnp.concatenate([jnp.cos(ang_r), jnp.sin(ang_r), jnp.cos(ang_c), jnp.sin(ang_c)], axis=1).T


def _window_bias(rel_bias):
    k = jnp.arange(3 * BLOCK)[:, None]
    q = jnp.arange(BLOCK)[None, :]
    rel = k - BLOCK - q
    tab = _bucket_lookup(rel_bias[:, :B_HEADS].astype(F32) * LOG2E, _t5_bucket(rel))
    tab = jnp.where((jnp.abs(rel) <= WINDOW)[None], tab, NEG_BIG)
    tab = tab.reshape(B_KV, B_HEADS // B_KV, 3 * BLOCK, BLOCK).transpose(0, 2, 1, 3)
    return tab.reshape(B_KV, 3 * BLOCK, (B_HEADS // B_KV) * BLOCK)


def _diff_bias(rel_bias, tq, tk):
    lo, hi, g = _near_offsets(tq, tk)
    table = rel_bias[:, B_HEADS:].astype(F32) * LOG2E
    d = jnp.arange(lo - g, hi + g + 1, g)[:, None, None]
    rel = d + jnp.arange(tk)[None, :, None] - jnp.arange(tq)[None, None, :]
    return _bucket_lookup(table, _t5_bucket(rel))


def _natural_keys(qkv, row0, lead):
    b, _, s = qkv.shape
    n = int(np.prod(lead))
    kt = qkv[:, row0:row0 + n * HEAD_DIM, :].reshape((b,) + tuple(lead) + (HEAD_DIM, s))
    return jnp.swapaxes(kt, -1, -2)


def kernel(x, rel_bias, norm_ffn1, w_ffn1_in, w_ffn1_out, norm_mix, w_in, qnorm_a, knorm_a, sink_b, lam_q1, lam_k1, lam_q2, lam_k2, subln_c, w_branch, w_out, norm_ffn2, w_ffn2_in, w_ffn2_out, norm_final):
    b, s, d = x.shape
    depth = w_in.shape[0]
    tm = min(512, s)
    tq = min(256, s)
    tk = min(512, s)

    rope_tab = _rope_table(s)
    win_bias = _window_bias(rel_bias)
    diff_tiles = _diff_bias(rel_bias, tq, tk)

    for l in range(depth):
        lambda_init = 0.8 - 0.6 * math.exp(-0.3 * l)
        x = _ffn(x, norm_ffn1[l], w_ffn1_in[l], w_ffn1_out[l], tm=tm)

        w_qkv_t = w_in[l][:, :QKV_ROWS].T.astype(BF16)
        w_gate = w_in[l][:, QKV_ROWS:].reshape(d, N_BRANCH, d).transpose(1, 0, 2).astype(BF16)
        qkv = _inproj(x, norm_mix[l], w_qkv_t, qnorm_a[l], knorm_a[l], rope_tab, tm=tm)

        ka = _natural_keys(qkv, KA0, (A_KV,))
        kb = _natural_keys(qkv, KB0, (B_KV,))
        kc = _natural_keys(qkv, KC0, (C_KV, 2))

        ya = _dense_attn(qkv, ka, q_row0=QA0, v_row0=VA0, dv=HEAD_DIM, tq=tq, tk=tk)
        sink = jnp.repeat(sink_b[l].astype(F32).reshape(B_KV, 1, B_HEADS // B_KV) * LOG2E, BLOCK, axis=2)
        yb = _window_attn(qkv, kb, win_bias, sink)
        lam_vecs = jnp.stack([lam_q1[l], lam_k1[l], lam_q2[l], lam_k2[l]]).astype(F32)
        yc = _dense_attn(qkv, kc, q_row0=QC0, v_row0=VC0, dv=C_V_DIM, tq=tq, tk=tk,
                         diff_args=(diff_tiles, lam_vecs, subln_c[l], lambda_init))

        x = _merge(x, norm_mix[l], ya, yb, yc, w_gate, w_branch[l].astype(BF16), w_out[l].astype(BF16), tm=tm)
        x = _ffn(x, norm_ffn2[l], w_ffn2_in[l], w_ffn2_out[l],
                 final_gain=norm_final if l == depth - 1 else None, tm=tm)
    return x
```

```python
import functools
import math

import jax
import jax.numpy as jnp
import numpy as np
from jax import lax
from jax.experimental import pallas as pl
from jax.experimental.pallas import tpu as pltpu

F32 = jnp.float32
BF16 = jnp.bfloat16

HEAD_DIM = 64
A_HEADS, A_KV = 8, 2
B_HEADS, B_KV = 8, 2
C_HEADS, C_KV = 4, 2
C_V_DIM = 2 * HEAD_DIM
WINDOW = 128
BLOCK = 128
GRID_W = 64
ROPE_THETA = 10000.0
NUM_BUCKETS = 32
MAX_DISTANCE = 128
N_BRANCH = 3
BRANCH_WIDTH = 512
EPS = 1e-6
NEG_BIG = -1e30
LOG2E = math.log2(math.e)
SCALE = HEAD_DIM ** -0.5

QA0, KA0, VA0 = 0, 512, 640
QB0, KB0, VB0 = 768, 1280, 1408
QC0, KC0, VC0 = 1536, 2048, 2304
QKV_ROWS = 2560

SKEW = 2
SAFE_BOUND = 40.0
BOUND_SLACK = 1.0 + 2.0 ** -10
ONES_ROWS = 16
VMEM_LIMIT_BYTES = 56 * 1024 * 1024


def _params(semantics):
    return pltpu.CompilerParams(dimension_semantics=semantics, vmem_limit_bytes=VMEM_LIMIT_BYTES)


def _const_spec(shape):
    return pl.BlockSpec(shape, lambda *_: (0,) * len(shape), pipeline_mode=pl.Buffered(1))


def _rms_rows(x, gain):
    r = lax.rsqrt(jnp.mean(x * x, axis=-1, keepdims=True) + EPS)
    return x * r * gain


def _ffn_body(x_ref, g_ref, wg_ref, wu_ref, wo_ref, gf_ref, o_ref, *, chunks, final_norm):
    x = x_ref[0]
    h = _rms_rows(x, g_ref[...]).astype(BF16)
    acc = None
    for c0, cw in chunks:
        gg = jnp.dot(h, wg_ref[:, c0:c0 + cw], preferred_element_type=F32)
        uu = jnp.dot(h, wu_ref[:, c0:c0 + cw], preferred_element_type=F32)
        act = (gg * (1.0 / (1.0 + jnp.exp(-gg))) * uu).astype(BF16)
        part = jnp.dot(act, wo_ref[c0:c0 + cw, :], preferred_element_type=F32)
        acc = part if acc is None else acc + part
    y = x + 0.5 * acc
    if final_norm:
        y = _rms_rows(y, gf_ref[...])
    o_ref[0] = y


def _ffn(x, gain, w_in, w_out, final_gain=None, *, tm):
    b, s, d = x.shape
    f = w_out.shape[0]
    wg = w_in[:, :f].astype(BF16)
    wu = w_in[:, f:].astype(BF16)
    wo = w_out.astype(BF16)
    half = f // 2
    chunks = ((0, half), (half, f - half)) if half % 128 == 0 else ((0, f),)
    final_norm = final_gain is not None
    gf = (final_gain if final_norm else gain).reshape(1, d)
    const = _const_spec
    return pl.pallas_call(
        functools.partial(_ffn_body, chunks=chunks, final_norm=final_norm),
        grid=(b, s // tm),
        in_specs=[
            pl.BlockSpec((1, tm, d), lambda bi, i: (bi, i, 0)),
            const((1, d)), const((d, f)), const((d, f)), const((f, d)), const((1, d)),
        ],
        out_specs=pl.BlockSpec((1, tm, d), lambda bi, i: (bi, i, 0)),
        out_shape=jax.ShapeDtypeStruct((b, s, d), F32),
        compiler_params=_params(("parallel", "parallel")),
        name="ffn",
    )(x, gain.reshape(1, d), wg, wu, wo, gf)


def _rope_rows(x, tab):
    cr, sr, cc, sc = tab[0:16], tab[16:32], tab[32:48], tab[48:64]
    x1r, x2r, x1c, x2c = x[0:16], x[16:32], x[32:48], x[48:64]
    return jnp.concatenate(
        [x1r * cr - x2r * sr, x1r * sr + x2r * cr, x1c * cc - x2c * sc, x1c * sc + x2c * cc], axis=0)


def _head_norm_rope(x, gain, tab):
    r = lax.rsqrt(jnp.mean(x * x, axis=0, keepdims=True) + EPS)
    return _rope_rows(x * r * gain, tab)


def _inproj_body(x_ref, g_ref, wt_ref, qg_ref, kg_ref, tab_ref, o_ref):
    h = _rms_rows(x_ref[0], g_ref[...]).astype(BF16)
    yt = lax.dot_general(wt_ref[...], h, (((1,), (1,)), ((), ())),
                         preferred_element_type=F32)
    tab = tab_ref[...]
    qscale = SCALE * LOG2E
    for hd in range(A_HEADS):
        r0 = QA0 + hd * HEAD_DIM
        q = _head_norm_rope(yt[r0:r0 + HEAD_DIM], qg_ref[...], tab)
        o_ref[0, r0:r0 + HEAD_DIM, :] = (q * qscale).astype(BF16)
    for hd in range(A_KV):
        r0 = KA0 + hd * HEAD_DIM
        o_ref[0, r0:r0 + HEAD_DIM, :] = _head_norm_rope(yt[r0:r0 + HEAD_DIM], kg_ref[...], tab).astype(BF16)
    o_ref[0, VA0:QB0, :] = yt[VA0:QB0].astype(BF16)
    o_ref[0, QB0:KB0, :] = (yt[QB0:KB0] * qscale).astype(BF16)
    o_ref[0, KB0:QC0, :] = yt[KB0:QC0].astype(BF16)
    o_ref[0, QC0:KC0, :] = (yt[QC0:KC0] * qscale).astype(BF16)
    o_ref[0, KC0:QKV_ROWS, :] = yt[KC0:QKV_ROWS].astype(BF16)


def _inproj(x, gain, w_qkv_t, qgain, kgain, rope_tab, *, tm):
    b, s, d = x.shape
    const = _const_spec
    return pl.pallas_call(
        _inproj_body,
        grid=(b, s // tm),
        in_specs=[
            pl.BlockSpec((1, tm, d), lambda bi, i: (bi, i, 0)),
            const((1, d)), const((QKV_ROWS, d)), const((HEAD_DIM, tm)), const((HEAD_DIM, tm)),
            pl.BlockSpec((HEAD_DIM, tm), lambda bi, i: (0, i)),
        ],
        out_specs=pl.BlockSpec((1, QKV_ROWS, tm), lambda bi, i: (bi, 0, i)),
        out_shape=jax.ShapeDtypeStruct((b, QKV_ROWS, s), BF16),
        compiler_params=_params(("parallel", "parallel")),
        name="inproj",
    )(x, gain.reshape(1, d), w_qkv_t,
      jnp.broadcast_to(qgain.reshape(HEAD_DIM, 1), (HEAD_DIM, tm)),
      jnp.broadcast_to(kgain.reshape(HEAD_DIM, 1), (HEAD_DIM, tm)),
      rope_tab)


def _dense_attn_body(*refs, dv, tq, tk, n_k, unroll, diff, near_lo, near_step, n_near, near_shift, lambda_init):
    near_pad = near_shift * near_step
    fast_unroll = 16 if n_k % 16 == 0 else unroll
    if diff:
        (q_ref, k_ref, v_ref, bnd_ref, bias_ref, lam_ref, sg_ref, o_ref,
         acc_ref, m_ref, s_ref, off_ref, alpha_ref) = refs
    else:
        (q_ref, k_ref, v_ref, bnd_ref, o_ref, acc_ref, m_ref, s_ref, off_ref, alpha_ref) = refs
    n_maps = 4
    bi = pl.program_id(0)
    kvh = pl.program_id(1)
    qi = pl.program_id(2)
    ones = jnp.ones((ONES_ROWS, tk), BF16)

    def scores(j, mp):
        ks = pl.multiple_of(j * tk, tk)
        if diff:
            kmat = k_ref[0, 0, mp % 2, pl.ds(ks, tk), :]
        else:
            kmat = k_ref[0, 0, pl.ds(ks, tk), :]
        s = jnp.dot(kmat, q_ref[0, mp * HEAD_DIM:(mp + 1) * HEAD_DIM, :], preferred_element_type=F32)
        if diff:
            di = jnp.clip(lax.div(j * tk - qi * tq - near_lo + near_pad, near_step) - near_shift + 1, 0, n_near + 1)
            s = s + bias_ref[mp // 2, di]
        return s

    def values(j):
        ks = max(j, 0) * tk if isinstance(j, int) else pl.multiple_of(jnp.maximum(j, 0) * tk, tk)
        return jnp.concatenate([v_ref[0, :, pl.ds(ks, tk)], ones], axis=0)

    acc_ref[...] = jnp.zeros(acc_ref.shape, F32)

    bound_max = None
    for mp in range(n_maps):
        q = q_ref[0, mp * HEAD_DIM:(mp + 1) * HEAD_DIM, :].astype(F32)
        qn = jnp.sqrt(jnp.sum(q * q, axis=0, keepdims=True))
        if diff:
            kmax = bnd_ref[(bi * 2 + kvh) * 2 + mp % 2]
            extra = bnd_ref[pl.num_programs(0) * 4 + kvh * 2 + mp // 2]
        else:
            kmax = bnd_ref[bi * 2 + kvh]
            extra = 0.0
        bound = qn * (kmax * BOUND_SLACK) + (extra + (BOUND_SLACK - 1.0))
        off_ref[0, mp] = bound
        tile_max = jnp.max(bound)
        bound_max = tile_max if bound_max is None else jnp.maximum(bound_max, tile_max)
    bounded = bound_max <= SAFE_BOUND

    @pl.when(bounded)
    def _():
        def body(i, carry):
            tiles = [(fast_unroll * i + u, mp) for u in range(fast_unroll) for mp in range(n_maps)]
            pending = [scores(*tiles[t]) for t in range(SKEW)]
            for t, (j, mp) in enumerate(tiles):
                if t + SKEW < len(tiles):
                    pending.append(scores(*tiles[t + SKEW]))
                p = jnp.exp2(pending.pop(0) - off_ref[0, mp]).astype(BF16)
                acc_ref[mp] += jnp.dot(values(j), p, preferred_element_type=F32)
            return carry
        lax.fori_loop(0, n_k // fast_unroll, body, 0)

    @pl.when(jnp.logical_not(bounded))
    def _():
        m_ref[...] = jnp.full(m_ref.shape, NEG_BIG, F32)
        s_ref[1] = jnp.zeros(s_ref.shape[1:], F32)
        off_ref[1] = jnp.full(off_ref.shape[1:], -NEG_BIG, F32)
        alpha_ref[1] = jnp.ones(alpha_ref.shape[1:], F32)

        def score(j, par):
            for mp in range(n_maps):
                s = scores(j, mp)
                m_old = m_ref[mp]
                m_new = jnp.maximum(m_old, jnp.max(s, axis=0, keepdims=True))
                m_ref[mp] = m_new
                alpha_ref[par, mp] = jnp.exp2(m_old - m_new)
                off_ref[par, mp] = m_new
                s_ref[par, mp] = s

        def accumulate(j, par):
            v_aug = values(j)
            for mp in range(n_maps):
                p = jnp.exp2(s_ref[par, mp] - off_ref[par, mp]).astype(BF16)
                pv = jnp.dot(v_aug, p, preferred_element_type=F32)
                acc_ref[mp] = alpha_ref[par, mp] * acc_ref[mp] + pv

        def body(i, carry):
            for u in range(unroll):
                j = unroll * i + u
                score(j, u % 2)
                accumulate(j - 1, 1 - u % 2)
            return carry

        lax.fori_loop(0, n_k // unroll, body, 0)
        accumulate(n_k - 1, 1)

    if diff:
        lam = (jnp.exp(jnp.sum(lam_ref[0:1] * lam_ref[1:2], axis=-1, keepdims=True))
               - jnp.exp(jnp.sum(lam_ref[2:3] * lam_ref[3:4], axis=-1, keepdims=True)) + lambda_init)
        outs = []
        for hh in range(2):
            a0 = acc_ref[2 * hh]
            a1 = acc_ref[2 * hh + 1]
            o = a0[0:dv] / a0[dv:dv + 1] - lam * (a1[0:dv] / a1[dv:dv + 1])
            r = lax.rsqrt(jnp.mean(o * o, axis=0, keepdims=True) + EPS)
            outs.append(o * r * sg_ref[...] * (1.0 - lambda_init))
        ot = jnp.concatenate(outs, axis=0)
    else:
        outs = []
        for mp in range(n_maps):
            a = acc_ref[mp]
            outs.append(a[0:dv] / a[dv:dv + 1])
        ot = jnp.concatenate(outs, axis=0)
    o_ref[0] = ot.T.astype(BF16)


def _near_offsets(tq, tk):
    g = math.gcd(tq, tk)
    ds = [d for d in range(-(tk // g + 1) * g, (tq // g + 2) * g, g)
          if d - (tq - 1) <= MAX_DISTANCE - 1 and d + (tk - 1) >= -(MAX_DISTANCE - 1)]
    return ds[0], ds[-1], g


def _dense_attn(qkv, k_nat, *, q_row0, v_row0, dv, tq, tk, diff_args=None):
    b, _, s = qkv.shape
    n_k = s // tk
    unroll = 4 if n_k % 4 == 0 else 2
    assert n_k % unroll == 0
    diff = diff_args is not None
    qblk = q_row0 // 256
    vblk = v_row0 // dv
    once = pl.Buffered(1)
    in_specs = [
        pl.BlockSpec((1, 256, tq), lambda bi, g, i: (bi, qblk + g, i)),
        (pl.BlockSpec((1, 1, 2, s, HEAD_DIM), lambda bi, g, i: (bi, g, 0, 0, 0), pipeline_mode=once) if diff
         else pl.BlockSpec((1, 1, s, HEAD_DIM), lambda bi, g, i: (bi, g, 0, 0), pipeline_mode=once)),
        pl.BlockSpec((1, dv, s), lambda bi, g, i: (bi, vblk + g, 0), pipeline_mode=once),
    ]
    kmax = jnp.sqrt(jnp.max(jnp.sum(jnp.square(k_nat.astype(F32)), axis=-1), axis=-1)).reshape(-1)
    in_specs.append(pl.BlockSpec(memory_space=pltpu.SMEM))
    args = [qkv, k_nat, qkv, kmax]
    kw = dict(near_lo=0, near_step=1, n_near=0, near_shift=0, lambda_init=0.0)
    if diff:
        bias_tiles, lam_vecs, subln, lambda_init = diff_args
        args[3] = jnp.concatenate([kmax, jnp.max(jnp.abs(bias_tiles), axis=(1, 2, 3))])
        lo, hi, g_ = _near_offsets(tq, tk)
        n_d = (hi - lo) // g_ + 1
        kw = dict(near_lo=lo, near_step=g_, n_near=n_d, near_shift=(s + 2 * tk) // g_ + 1, lambda_init=lambda_init)
        in_specs += [
            pl.BlockSpec((2, n_d + 2, tk, tq), lambda bi, g, i: (g, 0, 0, 0), pipeline_mode=once),
            _const_spec((4, HEAD_DIM)),
            _const_spec((dv, tq)),
        ]
        args += [bias_tiles, lam_vecs, jnp.broadcast_to(subln.reshape(dv, 1), (dv, tq))]
    return pl.pallas_call(
        functools.partial(_dense_attn_body, dv=dv, tq=tq, tk=tk, n_k=n_k, unroll=unroll, diff=diff, **kw),
        grid=(b, 2, s // tq),
        in_specs=in_specs,
        out_specs=pl.BlockSpec((1, tq, 256), lambda bi, g, i: (bi, i, g)),
        out_shape=jax.ShapeDtypeStruct((b, s, BRANCH_WIDTH), BF16),
        scratch_shapes=[
            pltpu.VMEM((4, dv + ONES_ROWS, tq), F32),
            pltpu.VMEM((4, 1, tq), F32),
            pltpu.VMEM((2, 4, tk, tq), F32),
            pltpu.VMEM((2, 4, 1, tq), F32),
            pltpu.VMEM((2, 4, 1, tq), F32),
        ],
        compiler_params=_params(("parallel", "parallel", "parallel")),
        name="attn_diff" if diff else "attn_axial",
    )(*args)


def _window_body(q_ref, kp_ref, kc_ref, kn_ref, vp_ref, vc_ref, vn_ref, bias_ref, sink_ref, o_ref, *, n_blocks):
    n = pl.program_id(2)
    qt = q_ref[0]
    q_all = jnp.concatenate([qt[h * HEAD_DIM:(h + 1) * HEAD_DIM] for h in range(4)], axis=1)
    kband = jnp.concatenate([kp_ref[0, 0], kc_ref[0, 0], kn_ref[0, 0]], axis=0)
    vband = jnp.concatenate([vp_ref[0], vc_ref[0], vn_ref[0]], axis=1)
    s = jnp.dot(kband, q_all, preferred_element_type=F32) + bias_ref[0]
    row = lax.broadcasted_iota(jnp.int32, s.shape, 0)
    valid = jnp.logical_and(jnp.logical_or(n > 0, row >= BLOCK),
                            jnp.logical_or(n < n_blocks - 1, row < 2 * BLOCK))
    s = jnp.where(valid, s, NEG_BIG)
    sk = sink_ref[0]
    m = jnp.maximum(jnp.max(s, axis=0, keepdims=True), sk)
    e = jnp.exp2(s - m)
    denom = jnp.sum(e, axis=0, keepdims=True) + jnp.exp2(sk - m)
    p = (e / denom).astype(BF16)
    outs = [jnp.dot(vband, p[:, h * BLOCK:(h + 1) * BLOCK], preferred_element_type=F32) for h in range(4)]
    o_ref[0] = jnp.concatenate(outs, axis=0).T.astype(BF16)


def _window_attn(qkv, k_nat, bias, sink):
    b, _, s = qkv.shape
    nb = s // BLOCK
    qblk = QB0 // 256
    vblk = VB0 // HEAD_DIM
    prev = lambda n: jnp.maximum(n - 1, 0)
    nxt = lambda n: jnp.minimum(n + 1, nb - 1)
    kspec = lambda f: pl.BlockSpec((1, 1, BLOCK, HEAD_DIM), lambda bi, g, n: (bi, g, f(n), 0))
    vspec = lambda f: pl.BlockSpec((1, HEAD_DIM, BLOCK), lambda bi, g, n: (bi, vblk + g, f(n)))
    same = lambda n: n
    return pl.pallas_call(
        functools.partial(_window_body, n_blocks=nb),
        grid=(b, 2, nb),
        in_specs=[
            pl.BlockSpec((1, 256, BLOCK), lambda bi, g, n: (bi, qblk + g, n)),
            kspec(prev), kspec(same), kspec(nxt),
            vspec(prev), vspec(same), vspec(nxt),
            pl.BlockSpec((1, 3 * BLOCK, 4 * BLOCK), lambda bi, g, n: (g, 0, 0)),
            pl.BlockSpec((1, 1, 4 * BLOCK), lambda bi, g, n: (g, 0, 0)),
        ],
        out_specs=pl.BlockSpec((1, BLOCK, 256), lambda bi, g, n: (bi, n, g)),
        out_shape=jax.ShapeDtypeStruct((b, s, BRANCH_WIDTH), BF16),
        compiler_params=_params(("parallel", "parallel", "parallel")),
        name="attn_window",
    )(qkv, k_nat, k_nat, k_nat, qkv, qkv, qkv, bias, sink)


def _merge_body(x_ref, g_ref, ya_ref, yb_ref, yc_ref, wgate_ref, wb_ref, wo_ref, o_ref):
    x = x_ref[0]
    h = _rms_rows(x, g_ref[...]).astype(BF16)
    merged = None
    for n, y_ref in enumerate((ya_ref, yb_ref, yc_ref)):
        logits = jnp.dot(h, wgate_ref[n], preferred_element_type=F32)
        gate = 1.0 / (1.0 + jnp.exp(-logits))
        branch = jnp.dot(y_ref[0], wb_ref[n], preferred_element_type=F32)
        term = gate * branch
        merged = term if merged is None else merged + term
    o_ref[0] = x + jnp.dot(merged.astype(BF16), wo_ref[...], preferred_element_type=F32)


def _merge(x, gain, ya, yb, yc, w_gate, w_branch, w_out, *, tm):
    b, s, d = x.shape
    const = _const_spec
    yspec =pl.BlockSpec((1, tm, BRANCH_WIDTH), lambda bi, i: (bi, i, 0))
    return pl.pallas_call(
        _merge_body,
        grid=(b, s // tm),
        in_specs=[
            pl.BlockSpec((1, tm, d), lambda bi, i: (bi, i, 0)),
            const((1, d)), yspec, yspec, yspec,
            const((N_BRANCH, d, d)), const((N_BRANCH, BRANCH_WIDTH, d)), const((d, d)),
        ],
        out_specs=pl.BlockSpec((1, tm, d), lambda bi, i: (bi, i, 0)),
        out_shape=jax.ShapeDtypeStruct((b, s, d), F32),
        compiler_params=_params(("parallel", "parallel")),
        name="merge",
    )(x, gain.reshape(1, d), ya, yb, yc, w_gate, w_branch, w_out)


def _t5_bucket(rel):
    nb = NUM_BUCKETS // 2
    max_exact = nb // 2
    side = jnp.where(rel > 0, nb, 0)
    n = jnp.abs(rel)
    nf = jnp.maximum(n, 1).astype(F32)
    large = max_exact + (jnp.log(nf / max_exact) / math.log(MAX_DISTANCE / max_exact) * (nb - max_exact)).astype(jnp.int32)
    large = jnp.minimum(large, nb - 1)
    return side + jnp.where(n < max_exact, n, large)


def _bucket_lookup(table, bucket):
    expand = (slice(None),) + (None,) * bucket.ndim
    out = jnp.zeros((table.shape[1],) + bucket.shape, F32)
    for bk in range(NUM_BUCKETS):
        out = jnp.where(bucket[None] == bk, table[bk][expand], out)
    return out


def _rope_table(s):
    rows = s // GRID_W
    row_ids = jnp.repeat(jnp.arange(rows), GRID_W).astype(F32)
    col_ids = jnp.tile(jnp.arange(GRID_W), rows).astype(F32)
    half = HEAD_DIM // 2
    freqs = ROPE_THETA ** (-jnp.arange(0, half, 2, dtype=F32) / half)
    ang_r = row_ids[:, None] * freqs
    ang_c = col_ids[:, None] * freqs
    return jnp.concatenate([jnp.cos(ang_r), jnp.sin(ang_r), jnp.cos(ang_c), jnp.sin(ang_c)], axis=1).T


def _window_bias(rel_bias):
    k = jnp.arange(3 * BLOCK)[:, None]
    q = jnp.arange(BLOCK)[None, :]
    rel = k - BLOCK - q
    tab = _bucket_lookup(rel_bias[:, :B_HEADS].astype(F32) * LOG2E, _t5_bucket(rel))
    tab = jnp.where((jnp.abs(rel) <= WINDOW)[None], tab, NEG_BIG)
    tab = tab.reshape(B_KV, B_HEADS // B_KV, 3 * BLOCK, BLOCK).transpose(0, 2, 1, 3)
    return tab.reshape(B_KV, 3 * BLOCK, (B_HEADS // B_KV) * BLOCK)


def _diff_bias(rel_bias, tq, tk):
    lo, hi, g = _near_offsets(tq, tk)
    table = rel_bias[:, B_HEADS:].astype(F32) * LOG2E
    d = jnp.arange(lo - g, hi + g + 1, g)[:, None, None]
    rel = d + jnp.arange(tk)[None, :, None] - jnp.arange(tq)[None, None, :]
    return _bucket_lookup(table, _t5_bucket(rel))


def _natural_keys(qkv, row0, lead):
    b, _, s = qkv.shape
    n = int(np.prod(lead))
    kt = qkv[:, row0:row0 + n * HEAD_DIM, :].reshape((b,) + tuple(lead) + (HEAD_DIM, s))
    return jnp.swapaxes(kt, -1, -2)


def kernel(x, rel_bias, norm_ffn1, w_ffn1_in, w_ffn1_out, norm_mix, w_in, qnorm_a, knorm_a, sink_b, lam_q1, lam_k1, lam_q2, lam_k2, subln_c, w_branch, w_out, norm_ffn2, w_ffn2_in, w_ffn2_out, norm_final):
    b, s, d = x.shape
    depth = w_in.shape[0]
    tm = min(512, s)
    tq = min(256, s)
    tk = min(512, s)

    rope_tab = _rope_table(s)
    win_bias = _window_bias(rel_bias)
    diff_tiles = _diff_bias(rel_bias, tq, tk)

    for l in range(depth):
        lambda_init = 0.8 - 0.6 * math.exp(-0.3 * l)
        x = _ffn(x, norm_ffn1[l], w_ffn1_in[l], w_ffn1_out[l], tm=tm)

        w_qkv_t = w_in[l][:, :QKV_ROWS].T.astype(BF16)
        w_gate = w_in[l][:, QKV_ROWS:].reshape(d, N_BRANCH, d).transpose(1, 0, 2).astype(BF16)
        qkv = _inproj(x, norm_mix[l], w_qkv_t, qnorm_a[l], knorm_a[l], rope_tab, tm=tm)

        ka = _natural_keys(qkv, KA0, (A_KV,))
        kb = _natural_keys(qkv, KB0, (B_KV,))
        kc = _natural_keys(qkv, KC0, (C_KV, 2))

        ya = _dense_attn(qkv, ka, q_row0=QA0, v_row0=VA0, dv=HEAD_DIM, tq=tq, tk=tk)
        sink = jnp.repeat(sink_b[l].astype(F32).reshape(B_KV, 1, B_HEADS // B_KV) * LOG2E, BLOCK, axis=2)
        yb = _window_attn(qkv, kb, win_bias, sink)
        lam_vecs = jnp.stack([lam_q1[l], lam_k1[l], lam_q2[l], lam_k2[l]]).astype(F32)
        yc = _dense_attn(qkv, kc, q_row0=QC0, v_row0=VC0, dv=C_V_DIM, tq=tq, tk=tk,
                         diff_args=(diff_tiles, lam_vecs, subln_c[l], lambda_init))

        x = _merge(x, norm_mix[l], ya, yb, yc, w_gate, w_branch[l].astype(BF16), w_out[l].astype(BF16), tm=tm)
        x = _ffn(x, norm_ffn2[l], w_ffn2_in[l], w_ffn2_out[l],
                 final_gain=norm_final if l == depth - 1 else None, tm=tm)
    return x
```

```python
import functools
import math

import jax
import jax.numpy as jnp
import numpy as np
from jax import lax
from jax.experimental import pallas as pl
from jax.experimental.pallas import tpu as pltpu

F32 = jnp.float32
BF16 = jnp.bfloat16

HEAD_DIM = 64
A_HEADS, A_KV = 8, 2
B_HEADS, B_KV = 8, 2
C_HEADS, C_KV = 4, 2
C_V_DIM = 2 * HEAD_DIM
WINDOW = 128
BLOCK = 128
GRID_W = 64
ROPE_THETA = 10000.0
NUM_BUCKETS = 32
MAX_DISTANCE = 128
N_BRANCH = 3
BRANCH_WIDTH = 512
EPS = 1e-6
NEG_BIG = -1e30
LOG2E = math.log2(math.e)
SCALE = HEAD_DIM ** -0.5

QA0, KA0, VA0 = 0, 512, 640
QB0, KB0, VB0 = 768, 1280, 1408
QC0, KC0, VC0 = 1536, 2048, 2304
QKV_ROWS = 2560

SKEW = 2
SAFE_BOUND = 40.0
BOUND_SLACK = 1.0 + 2.0 ** -10
ONES_ROWS = 16
VMEM_LIMIT_BYTES = 56 * 1024 * 1024


def _params(semantics):
    return pltpu.CompilerParams(dimension_semantics=semantics, vmem_limit_bytes=VMEM_LIMIT_BYTES)


def _const_spec(shape):
    return pl.BlockSpec(shape, lambda *_: (0,) * len(shape), pipeline_mode=pl.Buffered(1))


def _rms_rows(x, gain):
    r = lax.rsqrt(jnp.mean(x * x, axis=-1, keepdims=True) + EPS)
    return x * r * gain


def _ffn_body(x_ref, g_ref, wg_ref, wu_ref, wo_ref, gf_ref, o_ref, *, chunks, final_norm):
    x = x_ref[0]
    h = _rms_rows(x, g_ref[...]).astype(BF16)
    acc = None
    for c0, cw in chunks:
        gg = jnp.dot(h, wg_ref[:, c0:c0 + cw], preferred_element_type=F32)
        uu = jnp.dot(h, wu_ref[:, c0:c0 + cw], preferred_element_type=F32)
        act = (gg * (1.0 / (1.0 + jnp.exp(-gg))) * uu).astype(BF16)
        part = jnp.dot(act, wo_ref[c0:c0 + cw, :], preferred_element_type=F32)
        acc = part if acc is None else acc + part
    y = x + 0.5 * acc
    if final_norm:
        y = _rms_rows(y, gf_ref[...])
    o_ref[0] = y


def _ffn(x, gain, w_in, w_out, final_gain=None, *, tm):
    b, s, d = x.shape
    f = w_out.shape[0]
    wg = w_in[:, :f].astype(BF16)
    wu = w_in[:, f:].astype(BF16)
    wo = w_out.astype(BF16)
    half = f // 2
    chunks = ((0, half), (half, f - half)) if half % 128 == 0 else ((0, f),)
    final_norm = final_gain is not None
    gf = (final_gain if final_norm else gain).reshape(1, d)
    const = _const_spec
    return pl.pallas_call(
        functools.partial(_ffn_body, chunks=chunks, final_norm=final_norm),
        grid=(b, s // tm),
        in_specs=[
            pl.BlockSpec((1, tm, d), lambda bi, i: (bi, i, 0)),
            const((1, d)), const((d, f)), const((d, f)), const((f, d)), const((1, d)),
        ],
        out_specs=pl.BlockSpec((1, tm, d), lambda bi, i: (bi, i, 0)),
        out_shape=jax.ShapeDtypeStruct((b, s, d), F32),
        compiler_params=_params(("parallel", "parallel")),
        name="ffn",
    )(x, gain.reshape(1, d), wg, wu, wo, gf)


def _rope_rows(x, tab):
    cr, sr, cc, sc = tab[0:16], tab[16:32], tab[32:48], tab[48:64]
    x1r, x2r, x1c, x2c = x[0:16], x[16:32], x[32:48], x[48:64]
    return jnp.concatenate(
        [x1r * cr - x2r * sr, x1r * sr + x2r * cr, x1c * cc - x2c * sc, x1c * sc + x2c * cc], axis=0)


def _head_norm_rope(x, gain, tab):
    r = lax.rsqrt(jnp.mean(x * x, axis=0, keepdims=True) + EPS)
    return _rope_rows(x * r * gain, tab)


def _inproj_body(x_ref, g_ref, wt_ref, qg_ref, kg_ref, tab_ref, o_ref):
    h = _rms_rows(x_ref[0], g_ref[...]).astype(BF16)
    yt = lax.dot_general(wt_ref[...], h, (((1,), (1,)), ((), ())),
                         preferred_element_type=F32)
    tab = tab_ref[...]
    qscale = SCALE * LOG2E
    for hd in range(A_HEADS):
        r0 = QA0 + hd * HEAD_DIM
        q = _head_norm_rope(yt[r0:r0 + HEAD_DIM], qg_ref[...], tab)
        o_ref[0, r0:r0 + HEAD_DIM, :] = (q * qscale).astype(BF16)
    for hd in range(A_KV):
        r0 = KA0 + hd * HEAD_DIM
        o_ref[0, r0:r0 + HEAD_DIM, :] = _head_norm_rope(yt[r0:r0 + HEAD_DIM], kg_ref[...], tab).astype(BF16)
    o_ref[0, VA0:QB0, :] = yt[VA0:QB0].astype(BF16)
    o_ref[0, QB0:KB0, :] = (yt[QB0:KB0] * qscale).astype(BF16)
    o_ref[0, KB0:QC0, :] = yt[KB0:QC0].astype(BF16)
    o_ref[0, QC0:KC0, :] = (yt[QC0:KC0] * qscale).astype(BF16)
    o_ref[0, KC0:QKV_ROWS, :] = yt[KC0:QKV_ROWS].astype(BF16)


def _inproj(x, gain, w_qkv_t, qgain, kgain, rope_tab, *, tm):
    b, s, d = x.shape
    const = _const_spec
    return pl.pallas_call(
        _inproj_body,
        grid=(b, s // tm),
        in_specs=[
            pl.BlockSpec((1, tm, d), lambda bi, i: (bi, i, 0)),
            const((1, d)), const((QKV_ROWS, d)), const((HEAD_DIM, tm)), const((HEAD_DIM, tm)),
            pl.BlockSpec((HEAD_DIM, tm), lambda bi, i: (0, i)),
        ],
        out_specs=pl.BlockSpec((1, QKV_ROWS, tm), lambda bi, i: (bi, 0, i)),
        out_shape=jax.ShapeDtypeStruct((b, QKV_ROWS, s), BF16),
        compiler_params=_params(("parallel", "parallel")),
        name="inproj",
    )(x, gain.reshape(1, d), w_qkv_t,
      jnp.broadcast_to(qgain.reshape(HEAD_DIM, 1), (HEAD_DIM, tm)),
      jnp.broadcast_to(kgain.reshape(HEAD_DIM, 1), (HEAD_DIM, tm)),
      rope_tab)


def _dense_attn_body(*refs, dv, tq, tk, n_k, unroll, diff, near_lo, near_step, n_near, near_shift, lambda_init):
    near_pad = near_shift * near_step
    fast_unroll = 32 if n_k % 32 == 0 else (16 if n_k % 16 == 0 else unroll)
    if diff:
        (q_ref, k_ref, v_ref, bnd_ref, bias_ref, lam_ref, sg_ref, o_ref,
         acc_ref, m_ref, s_ref, off_ref, alpha_ref) = refs
    else:
        (q_ref, k_ref, v_ref, bnd_ref, o_ref, acc_ref, m_ref, s_ref, off_ref, alpha_ref) = refs
    n_maps = 4
    bi = pl.program_id(0)
    kvh = pl.program_id(1)
    qi = pl.program_id(2)
    ones = jnp.ones((ONES_ROWS, tk), BF16)

    def scores(j, mp):
        ks = pl.multiple_of(j * tk, tk)
        if diff:
            kmat = k_ref[0, 0, mp % 2, pl.ds(ks, tk), :]
        else:
            kmat = k_ref[0, 0, pl.ds(ks, tk), :]
        s = jnp.dot(kmat, q_ref[0, mp * HEAD_DIM:(mp + 1) * HEAD_DIM, :], preferred_element_type=F32)
        if diff:
            di = jnp.clip(lax.div(j * tk - qi * tq - near_lo + near_pad, near_step) - near_shift + 1, 0, n_near + 1)
            s = s + bias_ref[mp // 2, di]
        return s

    def values(j):
        ks = max(j, 0) * tk if isinstance(j, int) else pl.multiple_of(jnp.maximum(j, 0) * tk, tk)
        return jnp.concatenate([v_ref[0, :, pl.ds(ks, tk)], ones], axis=0)

    acc_ref[...] = jnp.zeros(acc_ref.shape, F32)

    bound_max = None
    for mp in range(n_maps):
        q = q_ref[0, mp * HEAD_DIM:(mp + 1) * HEAD_DIM, :].astype(F32)
        qn = jnp.sqrt(jnp.sum(q * q, axis=0, keepdims=True))
        if diff:
            kmax = bnd_ref[(bi * 2 + kvh) * 2 + mp % 2]
            extra = bnd_ref[pl.num_programs(0) * 4 + kvh * 2 + mp // 2]
        else:
            kmax = bnd_ref[bi * 2 + kvh]
            extra = 0.0
        bound = qn * (kmax * BOUND_SLACK) + (extra + (BOUND_SLACK - 1.0))
        off_ref[0, mp] = bound
        tile_max = jnp.max(bound)
        bound_max = tile_max if bound_max is None else jnp.maximum(bound_max, tile_max)
    bounded = bound_max <= SAFE_BOUND

    @pl.when(bounded)
    def _():
        def body(i, carry):
            tiles = [(fast_unroll * i + u, mp) for u in range(fast_unroll) for mp in range(n_maps)]
            pending = [scores(*tiles[t]) for t in range(SKEW)]
            for t, (j, mp) in enumerate(tiles):
                if t + SKEW < len(tiles):
                    pending.append(scores(*tiles[t + SKEW]))
                p = jnp.exp2(pending.pop(0) - off_ref[0, mp]).astype(BF16)
                acc_ref[mp] += jnp.dot(values(j), p, preferred_element_type=F32)
            return carry
        lax.fori_loop(0, n_k // fast_unroll, body, 0)

    @pl.when(jnp.logical_not(bounded))
    def _():
        m_ref[...] = jnp.full(m_ref.shape, NEG_BIG, F32)
        s_ref[1] = jnp.zeros(s_ref.shape[1:], F32)
        off_ref[1] = jnp.full(off_ref.shape[1:], -NEG_BIG, F32)
        alpha_ref[1] = jnp.ones(alpha_ref.shape[1:], F32)

        def score(j, par):
            for mp in range(n_maps):
                s = scores(j, mp)
                m_old = m_ref[mp]
                m_new = jnp.maximum(m_old, jnp.max(s, axis=0, keepdims=True))
                m_ref[mp] = m_new
                alpha_ref[par, mp] = jnp.exp2(m_old - m_new)
                off_ref[par, mp] = m_new
                s_ref[par, mp] = s

        def accumulate(j, par):
            v_aug = values(j)
            for mp in range(n_maps):
                p = jnp.exp2(s_ref[par, mp] - off_ref[par, mp]).astype(BF16)
                pv = jnp.dot(v_aug, p, preferred_element_type=F32)
                acc_ref[mp] = alpha_ref[par, mp] * acc_ref[mp] + pv

        def body(i, carry):
            for u in range(unroll):
                j = unroll * i + u
                score(j, u % 2)
                accumulate(j - 1, 1 - u % 2)
            return carry

        lax.fori_loop(0, n_k // unroll, body, 0)
        accumulate(n_k - 1, 1)

    if diff:
        lam = (jnp.exp(jnp.sum(lam_ref[0:1] * lam_ref[1:2], axis=-1, keepdims=True))
               - jnp.exp(jnp.sum(lam_ref[2:3] * lam_ref[3:4], axis=-1, keepdims=True)) + lambda_init)
        outs = []
        for hh in range(2):
            a0 = acc_ref[2 * hh]
            a1 = acc_ref[2 * hh + 1]
            o = a0[0:dv] / a0[dv:dv + 1] - lam * (a1[0:dv] / a1[dv:dv + 1])
            r = lax.rsqrt(jnp.mean(o * o, axis=0, keepdims=True) + EPS)
            outs.append(o * r * sg_ref[...] * (1.0 - lambda_init))
        ot = jnp.concatenate(outs, axis=0)
    else:
        outs = []
        for mp in range(n_maps):
            a = acc_ref[mp]
            outs.append(a[0:dv] / a[dv:dv + 1])
        ot = jnp.concatenate(outs, axis=0)
    o_ref[0] = ot.T.astype(BF16)


def _near_offsets(tq, tk):
    g = math.gcd(tq, tk)
    ds = [d for d in range(-(tk // g + 1) * g, (tq // g + 2) * g, g)
          if d - (tq - 1) <= MAX_DISTANCE - 1 and d + (tk - 1) >= -(MAX_DISTANCE - 1)]
    return ds[0], ds[-1], g


def _dense_attn(qkv, k_nat, *, q_row0, v_row0, dv, tq, tk, diff_args=None):
    b, _, s = qkv.shape
    n_k = s // tk
    unroll = 4 if n_k % 4 == 0 else 2
    assert n_k % unroll == 0
    diff = diff_args is not None
    qblk = q_row0 // 256
    vblk = v_row0 // dv
    once = pl.Buffered(1)
    in_specs = [
        pl.BlockSpec((1, 256, tq), lambda bi, g, i: (bi, qblk + g, i)),
        (pl.BlockSpec((1, 1, 2, s, HEAD_DIM), lambda bi, g, i: (bi, g, 0, 0, 0), pipeline_mode=once) if diff
         else pl.BlockSpec((1, 1, s, HEAD_DIM), lambda bi, g, i: (bi, g, 0, 0), pipeline_mode=once)),
        pl.BlockSpec((1, dv, s), lambda bi, g, i: (bi, vblk + g, 0), pipeline_mode=once),
    ]
    kmax = jnp.sqrt(jnp.max(jnp.sum(jnp.square(k_nat.astype(F32)), axis=-1), axis=-1)).reshape(-1)
    in_specs.append(pl.BlockSpec(memory_space=pltpu.SMEM))
    args = [qkv, k_nat, qkv, kmax]
    kw = dict(near_lo=0, near_step=1, n_near=0, near_shift=0, lambda_init=0.0)
    if diff:
        bias_tiles, lam_vecs, subln, lambda_init = diff_args
        args[3] = jnp.concatenate([kmax, jnp.max(jnp.abs(bias_tiles), axis=(1, 2, 3))])
        lo, hi, g_ = _near_offsets(tq, tk)
        n_d = (hi - lo) // g_ + 1
        kw = dict(near_lo=lo, near_step=g_, n_near=n_d, near_shift=(s + 2 * tk) // g_ + 1, lambda_init=lambda_init)
        in_specs += [
            pl.BlockSpec((2, n_d + 2, tk, tq), lambda bi, g, i: (g, 0, 0, 0), pipeline_mode=once),
            _const_spec((4, HEAD_DIM)),
            _const_spec((dv, tq)),
        ]
        args += [bias_tiles, lam_vecs, jnp.broadcast_to(subln.reshape(dv, 1), (dv, tq))]
    return pl.pallas_call(
        functools.partial(_dense_attn_body, dv=dv, tq=tq, tk=tk, n_k=n_k, unroll=unroll, diff=diff, **kw),
        grid=(b, 2, s // tq),
        in_specs=in_specs,
        out_specs=pl.BlockSpec((1, tq, 256), lambda bi, g, i: (bi, i, g)),
        out_shape=jax.ShapeDtypeStruct((b, s, BRANCH_WIDTH), BF16),
        scratch_shapes=[
            pltpu.VMEM((4, dv + ONES_ROWS, tq), F32),
            pltpu.VMEM((4, 1, tq), F32),
            pltpu.VMEM((2, 4, tk, tq), F32),
            pltpu.VMEM((2, 4, 1, tq), F32),
            pltpu.VMEM((2, 4, 1, tq), F32),
        ],
        compiler_params=_params(("parallel", "parallel", "parallel")),
        name="attn_diff" if diff else "attn_axial",
    )(*args)


def _window_body(q_ref, kp_ref, kc_ref, kn_ref, vp_ref, vc_ref, vn_ref, bias_ref, sink_ref, o_ref, *, n_blocks):
    n = pl.program_id(2)
    qt = q_ref[0]
    q_all = jnp.concatenate([qt[h * HEAD_DIM:(h + 1) * HEAD_DIM] for h in range(4)], axis=1)
    kband = jnp.concatenate([kp_ref[0, 0], kc_ref[0, 0], kn_ref[0, 0]], axis=0)
    vband = jnp.concatenate([vp_ref[0], vc_ref[0], vn_ref[0]], axis=1)
    s = jnp.dot(kband, q_all, preferred_element_type=F32) + bias_ref[0]
    row = lax.broadcasted_iota(jnp.int32, s.shape, 0)
    valid = jnp.logical_and(jnp.logical_or(n > 0, row >= BLOCK),
                            jnp.logical_or(n < n_blocks - 1, row < 2 * BLOCK))
    s = jnp.where(valid, s, NEG_BIG)
    sk = sink_ref[0]
    m = jnp.maximum(jnp.max(s, axis=0, keepdims=True), sk)
    e = jnp.exp2(s - m)
    denom = jnp.sum(e, axis=0, keepdims=True) + jnp.exp2(sk - m)
    p = (e / denom).astype(BF16)
    outs = [jnp.dot(vband, p[:, h * BLOCK:(h + 1) * BLOCK], preferred_element_type=F32) for h in range(4)]
    o_ref[0] = jnp.concatenate(outs, axis=0).T.astype(BF16)


def _window_attn(qkv, k_nat, bias, sink):
    b, _, s = qkv.shape
    nb = s // BLOCK
    qblk = QB0 // 256
    vblk = VB0 // HEAD_DIM
    prev = lambda n: jnp.maximum(n - 1, 0)
    nxt = lambda n: jnp.minimum(n + 1, nb - 1)
    kspec = lambda f: pl.BlockSpec((1, 1, BLOCK, HEAD_DIM), lambda bi, g, n: (bi, g, f(n), 0))
    vspec = lambda f: pl.BlockSpec((1, HEAD_DIM, BLOCK), lambda bi, g, n: (bi, vblk + g, f(n)))
    same = lambda n: n
    return pl.pallas_call(
        functools.partial(_window_body, n_blocks=nb),
        grid=(b, 2, nb),
        in_specs=[
            pl.BlockSpec((1, 256, BLOCK), lambda bi, g, n: (bi, qblk + g, n)),
            kspec(prev), kspec(same), kspec(nxt),
            vspec(prev), vspec(same), vspec(nxt),
            pl.BlockSpec((1, 3 * BLOCK, 4 * BLOCK), lambda bi, g, n: (g, 0, 0)),
            pl.BlockSpec((1, 1, 4 * BLOCK), lambda bi, g, n: (g, 0, 0)),
        ],
        out_specs=pl.BlockSpec((1, BLOCK, 256), lambda bi, g, n: (bi, n, g)),
        out_shape=jax.ShapeDtypeStruct((b, s, BRANCH_WIDTH), BF16),
        compiler_params=_params(("parallel", "parallel", "parallel")),
        name="attn_window",
    )(qkv, k_nat, k_nat, k_nat, qkv, qkv, qkv, bias, sink)


def _merge_body(x_ref, g_ref, ya_ref, yb_ref, yc_ref, wgate_ref, wb_ref, wo_ref, o_ref):
    x = x_ref[0]
    h = _rms_rows(x, g_ref[...]).astype(BF16)
    merged = None
    for n, y_ref in enumerate((ya_ref, yb_ref, yc_ref)):
        logits = jnp.dot(h, wgate_ref[n], preferred_element_type=F32)
        gate = 1.0 / (1.0 + jnp.exp(-logits))
        branch = jnp.dot(y_ref[0], wb_ref[n], preferred_element_type=F32)
        term = gate * branch
        merged = term if merged is None else merged + term
    o_ref[0] = x + jnp.dot(merged.astype(BF16), wo_ref[...], preferred_element_type=F32)


def _merge(x, gain, ya, yb, yc, w_gate, w_branch, w_out, *, tm):
    b, s, d = x.shape
    const = _const_spec
    yspec =pl.BlockSpec((1, tm, BRANCH_WIDTH), lambda bi, i: (bi, i, 0))
    return pl.pallas_call(
        _merge_body,
        grid=(b, s // tm),
        in_specs=[
            pl.BlockSpec((1, tm, d), lambda bi, i: (bi, i, 0)),
            const((1, d)), yspec, yspec, yspec,
            const((N_BRANCH, d, d)), const((N_BRANCH, BRANCH_WIDTH, d)), const((d, d)),
        ],
        out_specs=pl.BlockSpec((1, tm, d), lambda bi, i: (bi, i, 0)),
        out_shape=jax.ShapeDtypeStruct((b, s, d), F32),
        compiler_params=_params(("parallel", "parallel")),
        name="merge",
    )(x, gain.reshape(1, d), ya, yb, yc, w_gate, w_branch, w_out)


def _t5_bucket(rel):
    nb = NUM_BUCKETS // 2
    max_exact = nb // 2
    side = jnp.where(rel > 0, nb, 0)
    n = jnp.abs(rel)
    nf = jnp.maximum(n, 1).astype(F32)
    large = max_exact + (jnp.log(nf / max_exact) / math.log(MAX_DISTANCE / max_exact) * (nb - max_exact)).astype(jnp.int32)
    large = jnp.minimum(large, nb - 1)
    return side + jnp.where(n < max_exact, n, large)


def _bucket_lookup(table, bucket):
    expand = (slice(None),) + (None,) * bucket.ndim
    out = jnp.zeros((table.shape[1],) + bucket.shape, F32)
    for bk in range(NUM_BUCKETS):
        out = jnp.where(bucket[None] == bk, table[bk][expand], out)
    return out


def _rope_table(s):
    rows = s // GRID_W
    row_ids = jnp.repeat(jnp.arange(rows), GRID_W).astype(F32)
    col_ids = jnp.tile(jnp.arange(GRID_W), rows).astype(F32)
    half = HEAD_DIM // 2
    freqs = ROPE_THETA ** (-jnp.arange(0, half, 2, dtype=F32) / half)
    ang_r = row_ids[:, None] * freqs
    ang_c = col_ids[:, None] * freqs
    return jnp.concatenate([jnp.cos(ang_r), jnp.sin(ang_r), jnp.cos(ang_c), jnp.sin(ang_c)], axis=1).T


def _window_bias(rel_bias):
    k = jnp.arange(3 * BLOCK)[:, None]
    q = jnp.arange(BLOCK)[None, :]
    rel = k - BLOCK - q
    tab = _bucket_lookup(rel_bias[:, :B_HEADS].astype(F32) * LOG2E, _t5_bucket(rel))
    tab = jnp.where((jnp.abs(rel) <= WINDOW)[None], tab, NEG_BIG)
    tab = tab.reshape(B_KV, B_HEADS // B_KV, 3 * BLOCK, BLOCK).transpose(0, 2, 1, 3)
    return tab.reshape(B_KV, 3 * BLOCK, (B_HEADS // B_KV) * BLOCK)


def _diff_bias(rel_bias, tq, tk):
    lo, hi, g = _near_offsets(tq, tk)
    table = rel_bias[:, B_HEADS:].astype(F32) * LOG2E
    d = jnp.arange(lo - g, hi + g + 1, g)[:, None, None]
    rel = d + jnp.arange(tk)[None, :, None] - jnp.arange(tq)[None, None, :]
    return _bucket_lookup(table, _t5_bucket(rel))


def _natural_keys(qkv, row0, lead):
    b, _, s = qkv.shape
    n = int(np.prod(lead))
    kt = qkv[:, row0:row0 + n * HEAD_DIM, :].reshape((b,) + tuple(lead) + (HEAD_DIM, s))
    return jnp.swapaxes(kt, -1, -2)


def kernel(x, rel_bias, norm_ffn1, w_ffn1_in, w_ffn1_out, norm_mix, w_in, qnorm_a, knorm_a, sink_b, lam_q1, lam_k1, lam_q2, lam_k2, subln_c, w_branch, w_out, norm_ffn2, w_ffn2_in, w_ffn2_out, norm_final):
    b, s, d = x.shape
    depth = w_in.shape[0]
    tm = min(512, s)
    tq = min(256, s)
    tk = min(512, s)

    rope_tab = _rope_table(s)
    win_bias = _window_bias(rel_bias)
    diff_tiles = _diff_bias(rel_bias, tq, tk)

    for l in range(depth):
        lambda_init = 0.8 - 0.6 * math.exp(-0.3 * l)
        x = _ffn(x, norm_ffn1[l], w_ffn1_in[l], w_ffn1_out[l], tm=tm)

        w_qkv_t = w_in[l][:, :QKV_ROWS].T.astype(BF16)
        w_gate = w_in[l][:, QKV_ROWS:].reshape(d, N_BRANCH, d).transpose(1, 0, 2).astype(BF16)
        qkv = _inproj(x, norm_mix[l], w_qkv_t, qnorm_a[l], knorm_a[l], rope_tab, tm=tm)

        ka = _natural_keys(qkv, KA0, (A_KV,))
        kb = _natural_keys(qkv, KB0, (B_KV,))
        kc = _natural_keys(qkv, KC0, (C_KV, 2))

        ya = _dense_attn(qkv, ka, q_row0=QA0, v_row0=VA0, dv=HEAD_DIM, tq=tq, tk=tk)
        sink = jnp.repeat(sink_b[l].astype(F32).reshape(B_KV, 1, B_HEADS // B_KV) * LOG2E, BLOCK, axis=2)
        yb = _window_attn(qkv, kb, win_bias, sink)
        lam_vecs = jnp.stack([lam_q1[l], lam_k1[l], lam_q2[l], lam_k2[l]]).astype(F32)
        yc = _dense_attn(qkv, kc, q_row0=QC0, v_row0=VC0, dv=C_V_DIM, tq=tq, tk=tk,
                         diff_args=(diff_tiles, lam_vecs, subln_c[l], lambda_init))

        x = _merge(x, norm_mix[l], ya, yb, yc, w_gate, w_branch[l].astype(BF16), w_out[l].astype(BF16), tm=tm)
        x = _ffn(x, norm_ffn2[l], w_ffn2_in[l], w_ffn2_out[l],
                 final_gain=norm_final if l == depth - 1 else None, tm=tm)
    return x
```
